```python
import jax, jax.numpy as jnp
from jax import lax
import numpy as np

D_MODEL = 1024
BATCH = 2
SEQ = 8192
DEPTH = 4
DEC_BATCH = 128
DEC_SEQ = 1
PAST_LEN = 2048
PAGE_SIZE = 128

N_MIXERS = 2
N_META = 16
EPS = 1e-6
E_CONV = 2 * D_MODEL
CONV_W = 3
HEAD_DIM = 64
N_HEADS = D_MODEL // HEAD_DIM
ATT_W = N_HEADS * HEAD_DIM
Q_BLOCK = 128
SB_BIAS_INIT = -6.0
N_CONV_LAYERS = (DEPTH + 1) // N_MIXERS
N_ATT_LAYERS = DEPTH // N_MIXERS

kernel_name = "hybrid_shortconv_stickbreaking_step"


def rms_norm(x, g):
    xf = x.astype(jnp.float32)
    y = xf * lax.rsqrt(jnp.mean(xf * xf, axis=-1, keepdims=True) + EPS)
    return (y * g.astype(jnp.float32)).astype(x.dtype)


def conv_mixer(h, w_in, conv_w, w_out, past):
    T = h.shape[1]
    b, c, xin, z = jnp.split(h @ w_in, 4, axis=-1)
    u = c * xin
    upad = jnp.concatenate([past.astype(u.dtype), u], axis=1)
    conv = sum(conv_w[j] * upad[:, j:j + T] for j in range(CONV_W))
    y = (b * conv * jax.nn.silu(z)) @ w_out
    return y, upad[:, -(CONV_W - 1):]


def attn_in(h, w_in):
    N, T, _ = h.shape
    q, k, v, z = jnp.split(h @ w_in, 4, axis=-1)
    shp = (N, T, N_HEADS, HEAD_DIM)
    return q.reshape(shp), k.reshape(shp), v.reshape(shp), z


def sb_attend(q, k, v, q_pos, k_pos, bias):
    s = jnp.einsum('nqhd,nkhd->nhqk', q, k, preferred_element_type=jnp.float32) * (HEAD_DIM ** -0.5)
    s = s + bias.astype(jnp.float32)[None, :, None, None]
    mask = k_pos[None, :] < q_pos[:, None]
    log_beta = jax.nn.log_sigmoid(s)
    log_1mb = jnp.where(mask, jax.nn.log_sigmoid(-s), 0.0)
    after = lax.cumsum(log_1mb, axis=3, reverse=True) - log_1mb
    a = jnp.where(mask, jnp.exp(log_beta + after), 0.0)
    return jnp.einsum('nhqk,nkhd->nqhd', a.astype(v.dtype), v)


def sb_prompt(q, k, v, bias):
    N, L = q.shape[0], q.shape[1]
    pos = jnp.arange(L, dtype=jnp.int32)
    o_meta = sb_attend(q[:, :N_META], k[:, :N_META], v[:, :N_META], pos[:N_META], pos[:N_META], bias)
    nb = (L - N_META) // Q_BLOCK
    qb = q[:, N_META:].reshape(N, nb, Q_BLOCK, N_HEADS, HEAD_DIM).swapaxes(0, 1)
    qpos = pos[N_META:].reshape(nb, Q_BLOCK)
    o_b = lax.map(lambda a: sb_attend(a[0], k, v, a[1], pos, bias), (qb, qpos))
    o_real = o_b.swapaxes(0, 1).reshape(N, L - N_META, N_HEADS, HEAD_DIM)
    return jnp.concatenate([o_meta, o_real], axis=1)


def setup_inputs(seed: int = 0) -> dict:
    key = jax.random.key(seed)
    ks = jax.random.split(key, 16)
    n_pages = PAST_LEN // PAGE_SIZE
    n_used = DEC_BATCH * n_pages
    n_phys = n_used + n_used // 4
    f32 = jnp.float32
    x_prompt = jax.random.normal(ks[0], (BATCH, SEQ, D_MODEL), f32)
    x_sample = jax.random.normal(ks[1], (DEC_BATCH, DEC_SEQ, D_MODEL), f32)
    state_conv = jax.random.normal(ks[2], (N_CONV_LAYERS, DEC_BATCH, CONV_W - 1, E_CONV), f32)
    cache_k = jax.random.normal(ks[3], (N_ATT_LAYERS, n_phys, PAGE_SIZE, N_HEADS, HEAD_DIM), f32)
    cache_v = jax.random.normal(ks[4], (N_ATT_LAYERS, n_phys, PAGE_SIZE, N_HEADS, HEAD_DIM), f32)
    page_table = jax.random.permutation(ks[5], n_phys)[:n_used].reshape(DEC_BATCH, n_pages).astype(jnp.int32)
    meta_tokens = jax.random.normal(ks[6], (N_META, D_MODEL), f32)
    norm_g = 1.0 + 0.02 * jax.random.normal(ks[7], (DEPTH, D_MODEL), f32)
    final_norm_g = 1.0 + 0.02 * jax.random.normal(ks[8], (D_MODEL,), f32)
    w_in_conv = jax.random.normal(ks[9], (N_CONV_LAYERS, D_MODEL, 4 * E_CONV), f32) * D_MODEL ** -0.5
    conv_w = jax.random.normal(ks[10], (N_CONV_LAYERS, CONV_W, E_CONV), f32) * CONV_W ** -0.5
    w_out_conv = jax.random.normal(ks[11], (N_CONV_LAYERS, E_CONV, D_MODEL), f32) * E_CONV ** -0.5
    w_in_attn = jax.random.normal(ks[12], (N_ATT_LAYERS, D_MODEL, 4 * ATT_W), f32) * D_MODEL ** -0.5
    w_out_attn = jax.random.normal(ks[13], (N_ATT_LAYERS, ATT_W, D_MODEL), f32) * ATT_W ** -0.5
    sb_bias = SB_BIAS_INIT + 0.1 * jax.random.normal(ks[14], (N_ATT_LAYERS, N_HEADS), f32)
    return {"x_prompt": x_prompt, "x_sample": x_sample, "state_conv": state_conv, "cache_k": cache_k,
            "cache_v": cache_v, "page_table": page_table, "meta_tokens": meta_tokens, "norm_g": norm_g,
            "final_norm_g": final_norm_g, "w_in_conv": w_in_conv, "conv_w": conv_w, "w_out_conv": w_out_conv,
            "w_in_attn": w_in_attn, "w_out_attn": w_out_attn, "sb_bias": sb_bias}


def reference(x_prompt, x_sample, state_conv, cache_k, cache_v, page_table, meta_tokens, norm_g, final_norm_g,
              w_in_conv, conv_w, w_out_conv, w_in_attn, w_out_attn, sb_bias):
    B = x_prompt.shape[0]
    NB, T_new = x_sample.shape[0], x_sample.shape[1]
    past_len = page_table.shape[1] * cache_k.shape[2]
    meta = jnp.broadcast_to(meta_tokens[None].astype(x_prompt.dtype), (B, N_META, D_MODEL))
    xp = jnp.concatenate([meta, x_prompt], axis=1)
    xs = x_sample
    k_pos_s = jnp.arange(past_len + T_new, dtype=jnp.int32)
    q_pos_s = past_len + jnp.arange(T_new, dtype=jnp.int32)
    conv_p, conv_s, kp_l, vp_l, ks_l, vs_l = [], [], [], [], [], []
    for i in range(DEPTH):
        hp = rms_norm(xp, norm_g[i])
        hs = rms_norm(xs, norm_g[i])
        if i % N_MIXERS == 0:
            c = i // N_MIXERS
            zeros = jnp.zeros((B, CONV_W - 1, E_CONV), hp.dtype)
            op, stp = conv_mixer(hp, w_in_conv[c], conv_w[c], w_out_conv[c], zeros)
            os_, sts = conv_mixer(hs, w_in_conv[c], conv_w[c], w_out_conv[c], state_conv[c])
            conv_p.append(stp)
            conv_s.append(sts)
        else:
            a = i // N_MIXERS
            qp, kp, vp, zp = attn_in(hp, w_in_attn[a])
            ap = sb_prompt(qp, kp, vp, sb_bias[a]).reshape(B, -1, ATT_W)
            op = (ap * jax.nn.silu(zp)) @ w_out_attn[a]
            qs, kn, vn, zs = attn_in(hs, w_in_attn[a])
            k_past = cache_k[a][page_table].reshape(NB, past_len, N_HEADS, HEAD_DIM)
            v_past = cache_v[a][page_table].reshape(NB, past_len, N_HEADS, HEAD_DIM)
            k_all = jnp.concatenate([k_past.astype(kn.dtype), kn], axis=1)
            v_all = jnp.concatenate([v_past.astype(vn.dtype), vn], axis=1)
            a_s = sb_attend(qs, k_all, v_all, q_pos_s, k_pos_s, sb_bias[a]).reshape(NB, T_new, ATT_W)
            os_ = (a_s * jax.nn.silu(zs)) @ w_out_attn[a]
            kp_l.append(kp); vp_l.append(vp); ks_l.append(kn); vs_l.append(vn)
        xp = xp + op
        xs = xs + os_
    y_prompt = rms_norm(xp[:, N_META:], final_norm_g)
    y_sample = rms_norm(xs, final_norm_g)
    new_state_conv_prompt = jnp.stack(conv_p)
    new_state_conv_sample = jnp.stack(conv_s)
    new_k_prompt = jnp.stack(kp_l)
    new_v_prompt = jnp.stack(vp_l)
    new_k_sample = jnp.stack(ks_l)
    new_v_sample = jnp.stack(vs_l)
    return (y_prompt, y_sample, new_state_conv_prompt, new_state_conv_sample, new_k_prompt, new_v_prompt, new_k_sample, new_v_sample)
```

```python
import functools
import math

import jax
import jax.numpy as jnp
from jax import lax
from jax.experimental import pallas as pl
from jax.experimental.pallas import tpu as pltpu

D_MODEL = 1024
N_HEADS = 16
HEAD_DIM = 64
E_CONV = 2 * D_MODEL
CONV_W = 3
N_META = 16
EPS = 1e-6
LOG2E = 1.4426950408889634

SUBLANES = 8
LANES = 128
MXU_DIM = 256

ATT_TILE = MXU_DIM
ROW_TILE = 3 * ATT_TILE
IN_TILE = ATT_TILE
E_TILE = 512
QK_DIM = 2 * HEAD_DIM
VMEM_LIMIT = 56 * 1024 * 1024

_BF16 = jnp.bfloat16
_F32 = jnp.float32


def _params(n_axes):
    return pltpu.CompilerParams(
        dimension_semantics=("arbitrary",) * n_axes, vmem_limit_bytes=VMEM_LIMIT)


def _dot(a, b):
    return jnp.dot(a, b, preferred_element_type=_F32)


def _rms(x, g):
    ms = jnp.mean(x * x, axis=-1, keepdims=True)
    return x * lax.rsqrt(ms + EPS) * g


def _silu(z):
    return z * jax.nn.sigmoid(z)


def _softplus2(z):
    return jnp.maximum(z, 0.0) + jnp.log2(1.0 + jnp.exp2(-jnp.abs(z)))


def _conv_prompt_kernel(x_ref, g_ref, win_ref, cw_ref, wout_ref, xo_ref, st_ref,
                        h_s, acc_s, carry_s, *, n_e, st_row):
    i = pl.program_id(1)
    e = pl.program_id(2)

    @pl.when(e == 0)
    def _():
        h_s[...] = _rms(x_ref[...], g_ref[...]).astype(_BF16)
        acc_s[...] = jnp.zeros_like(acc_s)

    @pl.when(i == 0)
    def _():
        carry_s[e] = jnp.zeros(carry_s.shape[1:], _F32)

    h = h_s[...]
    b = _dot(h, win_ref[0])
    u = _dot(h, win_ref[1]) * _dot(h, win_ref[2])
    z = _dot(h, win_ref[3])

    prev = carry_s[e]
    row = lax.broadcasted_iota(jnp.int32, u.shape, 0)
    u1 = jnp.where(row == 0, prev[7:8, :], pltpu.roll(u, 1, 0))
    u2 = jnp.where(row == 0, prev[6:7, :],
                   jnp.where(row == 1, prev[7:8, :], pltpu.roll(u, 2, 0)))
    tm = u.shape[0]
    carry_s[e] = u[tm - SUBLANES:, :]
    st_ref[...] = u[st_row:st_row + SUBLANES, :]

    conv = cw_ref[0:1, :] * u2 + cw_ref[1:2, :] * u1 + cw_ref[2:3, :] * u
    y = (b * conv * _silu(z)).astype(_BF16)
    acc_s[...] += _dot(y, wout_ref[...])

    @pl.when(e == n_e - 1)
    def _():
        xo_ref[...] = x_ref[...] + acc_s[...]


def _conv_prompt(x, g, win4, cw, wout, seq_len):
    bsz, lp, d = x.shape
    e_dim = wout.shape[0]
    tm, te = ROW_TILE, E_TILE
    n_m, n_e = lp // tm, e_dim // te
    st_row = ((seq_len - 2) % tm) // SUBLANES * SUBLANES
    kern = functools.partial(_conv_prompt_kernel, n_e=n_e, st_row=st_row)
    x_new, st = pl.pallas_call(
        kern,
        grid=(bsz, n_m, n_e),
        in_specs=[
            pl.BlockSpec((None, tm, d), lambda b, i, e: (b, i, 0)),
            pl.BlockSpec((1, d), lambda b, i, e: (0, 0)),
            pl.BlockSpec((4, d, te), lambda b, i, e: (0, 0, e)),
            pl.BlockSpec((CONV_W, te), lambda b, i, e: (0, e)),
            pl.BlockSpec((te, d), lambda b, i, e: (e, 0)),
        ],
        out_specs=[
            pl.BlockSpec((None, tm, d), lambda b, i, e: (b, i, 0)),
            pl.BlockSpec((None, None, SUBLANES, te), lambda b, i, e: (b, i, 0, e)),
        ],
        out_shape=[
            jax.ShapeDtypeStruct((bsz, lp, d), _F32),
            jax.ShapeDtypeStruct((bsz, n_m, SUBLANES, e_dim), _F32),
        ],
        scratch_shapes=[
            pltpu.VMEM((tm, d), _BF16),
            pltpu.VMEM((tm, d), _F32),
            pltpu.VMEM((n_e, SUBLANES, te), _F32),
        ],
        compiler_params=_params(3),
        name="conv_prompt",
    )(x, g, win4, cw, wout)
    i_st = (seq_len - 2) // tm
    r = (seq_len - 2) % SUBLANES
    return x_new, st[:, i_st, r:r + 2, :]


def _conv_sample_kernel(x_ref, g_ref, win_ref, cw_ref, wout_ref, p0_ref, p1_ref,
                        xo_ref, u_ref, h_s, acc_s, *, n_e):
    e = pl.program_id(0)

    @pl.when(e == 0)
    def _():
        h_s[...] = _rms(x_ref[...], g_ref[...]).astype(_BF16)
        acc_s[...] = jnp.zeros_like(acc_s)

    h = h_s[...]
    b = _dot(h, win_ref[0])
    u = _dot(h, win_ref[1]) * _dot(h, win_ref[2])
    z = _dot(h, win_ref[3])
    u_ref[...] = u
    conv = cw_ref[0:1, :] * p0_ref[...] + cw_ref[1:2, :] * p1_ref[...] + cw_ref[2:3, :] * u
    y = (b * conv * _silu(z)).astype(_BF16)
    acc_s[...] += _dot(y, wout_ref[...])

    @pl.when(e == n_e - 1)
    def _():
        xo_ref[...] = x_ref[...] + acc_s[...]


def _conv_sample(x, g, win4, cw, wout, p0, p1):
    n, d = x.shape
    e_dim = wout.shape[0]
    te = E_TILE
    n_e = e_dim // te
    kern = functools.partial(_conv_sample_kernel, n_e=n_e)
    return pl.pallas_call(
        kern,
        grid=(n_e,),
        in_specs=[
            pl.BlockSpec((n, d), lambda e: (0, 0)),
            pl.BlockSpec((1, d), lambda e: (0, 0)),
            pl.BlockSpec((4, d, te), lambda e: (0, 0, e)),
            pl.BlockSpec((CONV_W, te), lambda e: (0, e)),
            pl.BlockSpec((te, d), lambda e: (e, 0)),
            pl.BlockSpec((n, te), lambda e: (0, e)),
            pl.BlockSpec((n, te), lambda e: (0, e)),
        ],
        out_specs=[
            pl.BlockSpec((n, d), lambda e: (0, 0)),
            pl.BlockSpec((n, te), lambda e: (0, e)),
        ],
        out_shape=[
            jax.ShapeDtypeStruct((n, d), _F32),
            jax.ShapeDtypeStruct((n, e_dim), _F32),
        ],
        scratch_shapes=[pltpu.VMEM((n, d), _BF16), pltpu.VMEM((n, d), _F32)],
        compiler_params=_params(1),
        name="conv_sample",
    )(x, g, win4, cw, wout, p0, p1)


def _attn_in_kernel(x_ref, g_ref, w_ref, kf_ref, vf_ref, qb_ref, kb_ref, vb_ref, z_ref):
    h = _rms(x_ref[...], g_ref[...]).astype(_BF16)
    q = _dot(h, w_ref[0])
    qb_ref[...] = (q * (HEAD_DIM ** -0.5 * LOG2E)).astype(_BF16)
    k = _dot(h, w_ref[1])
    kf_ref[...] = k
    kb_ref[...] = k.astype(_BF16)
    v = _dot(h, w_ref[2])
    vf_ref[...] = v
    vb_ref[...] = v.astype(_BF16)
    z_ref[...] = _dot(h, w_ref[3])


def _attn_in_prompt(x, g, w4, seq_len):
    bsz, lp, d = x.shape
    tm = IN_TILE
    row = lambda b, i: (b, i, 0)
    return pl.pallas_call(
        _attn_in_kernel,
        grid=(bsz, lp // tm),
        in_specs=[
            pl.BlockSpec((None, tm, d), row),
            pl.BlockSpec((1, d), lambda b, i: (0, 0)),
            pl.BlockSpec((4, d, d), lambda b, i: (0, 0, 0)),
        ],
        out_specs=[pl.BlockSpec((None, tm, d), row)] * 6,
        out_shape=[
            jax.ShapeDtypeStruct((bsz, seq_len, d), _F32),
            jax.ShapeDtypeStruct((bsz, seq_len, d), _F32),
            jax.ShapeDtypeStruct((bsz, lp, d), _BF16),
            jax.ShapeDtypeStruct((bsz, lp, d), _BF16),
            jax.ShapeDtypeStruct((bsz, lp, d), _BF16),
            jax.ShapeDtypeStruct((bsz, lp, d), _F32),
        ],
        compiler_params=_params(2),
        name="attn_in_prompt",
    )(x, g, w4)


def _attn_in_sample_kernel(x_ref, g_ref, w_ref, q_ref, k_ref, v_ref, z_ref):
    h = _rms(x_ref[...], g_ref[...]).astype(_BF16)
    q_ref[...] = _dot(h, w_ref[0]) * (HEAD_DIM ** -0.5 * LOG2E)
    k_ref[...] = _dot(h, w_ref[1])
    v_ref[...] = _dot(h, w_ref[2])
    z_ref[...] = _dot(h, w_ref[3])


def _attn_in_sample(x, g, w4):
    n, d = x.shape
    full = pl.BlockSpec((n, d), lambda i: (0, 0))
    return pl.pallas_call(
        _attn_in_sample_kernel,
        grid=(1,),
        in_specs=[full, pl.BlockSpec((1, d), lambda i: (0, 0)),
                  pl.BlockSpec((4, d, d), lambda i: (0, 0, 0))],
        out_specs=[full] * 4,
        out_shape=[jax.ShapeDtypeStruct((n, d), _F32)] * 4,
        compiler_params=_params(1),
        name="attn_in_sample",
    )(x, g, w4)


def _attn_out_kernel(o_ref, z_ref, x_ref, w_ref, xo_ref):
    y = (o_ref[...] * _silu(z_ref[...])).astype(_BF16)
    xo_ref[...] = x_ref[...] + _dot(y, w_ref[...])


def _attn_out(o, z, x, w, tm):
    rows, d = x.shape
    row = pl.BlockSpec((tm, d), lambda i: (i, 0))
    return pl.pallas_call(
        _attn_out_kernel,
        grid=(rows // tm,),
        in_specs=[row, row, row, pl.BlockSpec((d, d), lambda i: (0, 0))],
        out_specs=row,
        out_shape=jax.ShapeDtypeStruct((rows, d), _F32),
        compiler_params=_params(1),
        name="attn_out",
    )(o, z, x, w)


def _final_norm_kernel(x_ref, g_ref, y_ref):
    y_ref[...] = _rms(x_ref[...], g_ref[...])


def _final_norm(x, g, tm):
    rows, d = x.shape
    row = pl.BlockSpec((tm, d), lambda i: (i, 0))
    return pl.pallas_call(
        _final_norm_kernel,
        grid=(rows // tm,),
        in_specs=[row, pl.BlockSpec((1, d), lambda i: (0, 0))],
        out_specs=row,
        out_shape=jax.ShapeDtypeStruct((rows, d), _F32),
        compiler_params=_params(1),
        name="final_norm",
    )(x, g)


def _sb_prompt_kernel(kp_ref, qt_ref, vt_ref, u_ref, o_ref):
    qi = pl.program_id(2)
    tri = u_ref[...]
    tk = tri.shape[0]
    row = lax.broadcasted_iota(jnp.int32, (tk, tk), 0)
    col = lax.broadcasted_iota(jnp.int32, (tk, tk), 1)
    visible = row < col

    accs = []
    for hh in range(2):
        qt = qt_ref[hh]

        def tile(j, masked, hh=hh, qt=qt):
            z = _dot(kp_ref[hh, j], qt)
            sp = _softplus2(z)
            if masked:
                sp = jnp.where(visible, sp, 0.0)
            c = _dot(tri, sp.astype(_BF16))
            p = jnp.exp2(z - c)
            if masked:
                p = jnp.where(visible, p, 0.0)
            pv = _dot(vt_ref[hh, j], p.astype(_BF16))
            return pv, c[0:1, :]

        acc0, r0 = tile(qi, True)

        def body(it, carry, tile=tile):
            acc, r = carry
            pv, cs = tile(qi - 1 - it, False)
            return acc + jnp.exp2(-r) * pv, r + cs

        acc, _ = lax.fori_loop(0, qi, body, (acc0, r0))
        accs.append(acc)
    o_ref[...] = jnp.concatenate(accs, axis=0).T


def _sb_prompt(kp, qt, vt, tri):
    bsz, n_h, n_k, tk, qk = kp.shape
    lp = n_k * tk
    tq = tk
    return pl.pallas_call(
        _sb_prompt_kernel,
        grid=(bsz, n_h // 2, lp // tq),
        in_specs=[
            pl.BlockSpec((None, 2, n_k, tk, qk), lambda b, h, i: (b, h, 0, 0, 0)),
            pl.BlockSpec((None, 2, qk, tq), lambda b, h, i: (b, h, 0, i)),
            pl.BlockSpec((None, 2, n_k, HEAD_DIM, tk), lambda b, h, i: (b, h, 0, 0, 0)),
            pl.BlockSpec((tk, tk), lambda b, h, i: (0, 0)),
        ],
        out_specs=pl.BlockSpec((None, tq, 2 * HEAD_DIM), lambda b, h, i: (b, i, h)),
        out_shape=jax.ShapeDtypeStruct((bsz, lp, n_h * HEAD_DIM), _F32),
        compiler_params=_params(3),
        name="sb_prompt",
    )(kp, qt, vt, tri)


def _split3(x):
    hi = x.astype(_BF16)
    mid = (x - hi.astype(_F32)).astype(_BF16)
    lo = (x - hi.astype(_F32) - mid.astype(_F32)).astype(_BF16)
    return hi, mid, lo


def _prompt_attention_operands(qb, kb, vb, bias):
    bsz, lp, _ = qb.shape
    tk = ATT_TILE
    n_k = lp // tk
    pad = QK_DIM - HEAD_DIM - 3
    k4 = kb.reshape(bsz, n_k, tk, N_HEADS, HEAD_DIM).transpose(0, 3, 1, 2, 4)
    kp = jnp.concatenate([
        k4,
        jnp.ones(k4.shape[:-1] + (3,), _BF16),
        jnp.zeros(k4.shape[:-1] + (pad,), _BF16)], axis=-1)
    q4 = qb.reshape(bsz, lp, N_HEADS, HEAD_DIM).transpose(0, 2, 3, 1)
    parts = jnp.stack(_split3(bias.astype(_F32) * LOG2E), axis=-1)
    brow = jnp.broadcast_to(parts[None, :, :, None], (bsz, N_HEADS, 3, lp))
    qt = jnp.concatenate([q4, brow, jnp.zeros((bsz, N_HEADS, pad, lp), _BF16)], axis=2)
    vt = vb.reshape(bsz, n_k, tk, N_HEADS, HEAD_DIM).transpose(0, 3, 1, 4, 2)
    return kp, qt, vt


def _tri(n):
    r = lax.broadcasted_iota(jnp.int32, (n, n), 0)
    c = lax.broadcasted_iota(jnp.int32, (n, n), 1)
    return (c >= r).astype(_BF16)


def _sb_sample_kernel(pt_ref, *refs, n_pages):
    del pt_ref
    k_refs = refs[:n_pages]
    v_refs = refs[n_pages:2 * n_pages]
    qblk_ref, bias_ref, tri_ref, gt_ref, o_ref = refs[2 * n_pages:]
    qblk = qblk_ref[...]
    bias = bias_ref[...]
    tri = tri_ref[...]
    gt = gt_ref[...]
    page = k_refs[0].shape[0]
    r = jnp.zeros((1, N_HEADS), _F32)
    acc = jnp.zeros((SUBLANES, N_HEADS * HEAD_DIM), _F32)
    for p in reversed(range(n_pages)):
        z = _dot(k_refs[p][...].astype(_BF16), qblk) + bias
        c = _dot(tri, _softplus2(z).astype(_BF16))
        w = jnp.exp2(z - c - r)
        r = r + c[0:1, :]
        wx = _dot(w.astype(_BF16), gt)
        acc = acc + (wx * v_refs[p][...]).reshape(page // SUBLANES, SUBLANES, -1).sum(axis=0)
    o_ref[...] = jnp.sum(acc, axis=0, keepdims=True)


def _sb_sample(page_table, cache_k, cache_v, layer, qblk, bias2, tri, gt):
    n, n_pages = page_table.shape
    _, _, page, hd = cache_k.shape

    def page_spec(p):
        return pl.BlockSpec((None, None, page, hd), lambda b, pt, p=p: (layer, pt[b, p], 0, 0))

    in_specs = ([page_spec(p) for p in range(n_pages)] * 2 + [
        pl.BlockSpec((None, hd, N_HEADS), lambda b, pt: (b, 0, 0)),
        pl.BlockSpec((1, N_HEADS), lambda b, pt: (0, 0)),
        pl.BlockSpec((page, page), lambda b, pt: (0, 0)),
        pl.BlockSpec((N_HEADS, hd), lambda b, pt: (0, 0)),
    ])
    out = pl.pallas_call(
        functools.partial(_sb_sample_kernel, n_pages=n_pages),
        grid_spec=pltpu.PrefetchScalarGridSpec(
            num_scalar_prefetch=1,
            grid=(n,),
            in_specs=in_specs,
            out_specs=pl.BlockSpec((None, 1, hd), lambda b, pt: (b, 0, 0)),
        ),
        out_shape=jax.ShapeDtypeStruct((n, 1, hd), _F32),
        compiler_params=_params(1),
        name="sb_sample",
    )(page_table, *([cache_k] * n_pages), *([cache_v] * n_pages), qblk, bias2, tri, gt)
    return out.reshape(n, hd)


def kernel(x_prompt, x_sample, state_conv, cache_k, cache_v, page_table, meta_tokens, norm_g, final_norm_g,
           w_in_conv, conv_w, w_out_conv, w_in_attn, w_out_attn, sb_bias):
    bsz, seq, d = x_prompt.shape
    n_s = x_sample.shape[0]
    seq_len = N_META + seq
    lp = -(-seq_len // ROW_TILE) * ROW_TILE
    depth = norm_g.shape[0]

    meta = jnp.broadcast_to(meta_tokens[None].astype(x_prompt.dtype), (bsz, N_META, d))
    xp = jnp.concatenate([meta, x_prompt, jnp.zeros((bsz, lp - seq_len, d), x_prompt.dtype)], axis=1)
    xs = x_sample.reshape(n_s, d)

    n_phys, page = cache_k.shape[1], cache_k.shape[2]
    ck = cache_k.reshape(cache_k.shape[0], n_phys, page, N_HEADS * HEAD_DIM)
    cv = cache_v.reshape(cache_v.shape[0], n_phys, page, N_HEADS * HEAD_DIM)
    head_of = jnp.arange(N_HEADS * HEAD_DIM, dtype=jnp.int32) // HEAD_DIM
    blockmask = head_of[:, None] == jnp.arange(N_HEADS, dtype=jnp.int32)[None, :]
    gt = blockmask.T.astype(_BF16)
    tri_p = _tri(ATT_TILE)
    tri_s = _tri(page)

    conv_p, conv_s, kp_l, vp_l, ks_l, vs_l = [], [], [], [], [], []
    for i in range(depth):
        g = norm_g[i].reshape(1, d)
        if i % 2 == 0:
            c = i // 2
            win4 = w_in_conv[c].reshape(d, 4, E_CONV).transpose(1, 0, 2).astype(_BF16)
            wout = w_out_conv[c].astype(_BF16)
            xp, stp = _conv_prompt(xp, g, win4, conv_w[c], wout, seq_len)
            xs, u_s = _conv_sample(xs, g, win4, conv_w[c], wout, state_conv[c, :, 0, :], state_conv[c, :, 1, :])
            conv_p.append(stp)
            conv_s.append(jnp.stack([state_conv[c, :, 1, :], u_s], axis=1))
        else:
            a = i // 2
            w4 = w_in_attn[a].reshape(d, 4, d).transpose(1, 0, 2).astype(_BF16)
            wout = w_out_attn[a].astype(_BF16)
            kf, vf, qb, kb, vb, zp = _attn_in_prompt(xp, g, w4, seq_len)
            kp, qt, vt = _prompt_attention_operands(qb, kb, vb, sb_bias[a])
            op = _sb_prompt(kp, qt, vt, tri_p)
            xp = _attn_out(op.reshape(bsz * lp, d), zp.reshape(bsz * lp, d), xp.reshape(bsz * lp, d),
                           wout, ROW_TILE).reshape(bsz, lp, d)
            kp_l.append(kf.reshape(bsz, seq_len, N_HEADS, HEAD_DIM))
            vp_l.append(vf.reshape(bsz, seq_len, N_HEADS, HEAD_DIM))
            q2, kn, vn, zs = _attn_in_sample(xs, g, w4)
            qblk = jnp.where(blockmask[None], q2[:, :, None], 0.0).astype(_BF16)
            bias2 = (sb_bias[a].astype(_F32) * LOG2E).reshape(1, N_HEADS)
            os_ = _sb_sample(page_table, ck, cv, a, qblk, bias2, tri_s, gt)
            xs = _attn_out(os_, zs, xs, wout, n_s)
            ks_l.append(kn.reshape(n_s, 1, N_HEADS, HEAD_DIM))
            vs_l.append(vn.reshape(n_s, 1, N_HEADS, HEAD_DIM))

    gf = final_norm_g.reshape(1, d)
    y_prompt = _final_norm(xp.reshape(bsz * lp, d), gf, ROW_TILE).reshape(bsz, lp, d)[:, N_META:seq_len]
    y_sample = _final_norm(xs, gf, n_s).reshape(n_s, 1, d)
    return (y_prompt, y_sample, jnp.stack(conv_p), jnp.stack(conv_s),
            jnp.stack(kp_l), jnp.stack(vp_l), jnp.stack(ks_l), jnp.stack(vs_l))
```

```python
import functools
import math

import jax
import jax.numpy as jnp
from jax import lax
from jax.experimental import pallas as pl
from jax.experimental.pallas import tpu as pltpu

D_MODEL = 1024
N_HEADS = 16
HEAD_DIM = 64
E_CONV = 2 * D_MODEL
CONV_W = 3
N_META = 16
EPS = 1e-6
LOG2E = 1.4426950408889634

SUBLANES = 8
LANES = 128
MXU_DIM = 256

ATT_TILE = MXU_DIM
ROW_TILE = 3 * ATT_TILE
IN_TILE = ATT_TILE
E_TILE = 512
QK_DIM = 2 * HEAD_DIM
TILES_PER_STEP = 4
VMEM_LIMIT = 56 * 1024 * 1024

_BF16 = jnp.bfloat16
_F32 = jnp.float32


def _params(n_axes):
    return pltpu.CompilerParams(
        dimension_semantics=("arbitrary",) * n_axes, vmem_limit_bytes=VMEM_LIMIT)


def _dot(a, b):
    return jnp.dot(a, b, preferred_element_type=_F32)


def _rms(x, g):
    ms = jnp.mean(x * x, axis=-1, keepdims=True)
    return x * lax.rsqrt(ms + EPS) * g


def _silu(z):
    return z * jax.nn.sigmoid(z)


def _softplus2(z):
    return jnp.maximum(z, 0.0) + jnp.log2(1.0 + jnp.exp2(-jnp.abs(z)))


def _conv_prompt_kernel(x_ref, g_ref, win_ref, cw_ref, wout_ref, xo_ref, st_ref,
                        h_s, acc_s, carry_s, *, n_e, st_row):
    i = pl.program_id(1)
    e = pl.program_id(2)

    @pl.when(e == 0)
    def _():
        h_s[...] = _rms(x_ref[...], g_ref[...]).astype(_BF16)
        acc_s[...] = jnp.zeros_like(acc_s)

    @pl.when(i == 0)
    def _():
        carry_s[e] = jnp.zeros(carry_s.shape[1:], _F32)

    h = h_s[...]
    b = _dot(h, win_ref[0])
    u = _dot(h, win_ref[1]) * _dot(h, win_ref[2])
    z = _dot(h, win_ref[3])

    prev = carry_s[e]
    row = lax.broadcasted_iota(jnp.int32, u.shape, 0)
    u1 = jnp.where(row == 0, prev[7:8, :], pltpu.roll(u, 1, 0))
    u2 = jnp.where(row == 0, prev[6:7, :],
                   jnp.where(row == 1, prev[7:8, :], pltpu.roll(u, 2, 0)))
    tm = u.shape[0]
    carry_s[e] = u[tm - SUBLANES:, :]
    st_ref[...] = u[st_row:st_row + SUBLANES, :]

    conv = cw_ref[0:1, :] * u2 + cw_ref[1:2, :] * u1 + cw_ref[2:3, :] * u
    y = (b * conv * _silu(z)).astype(_BF16)
    acc_s[...] += _dot(y, wout_ref[...])

    @pl.when(e == n_e - 1)
    def _():
        xo_ref[...] = x_ref[...] + acc_s[...]


def _conv_prompt(x, g, win4, cw, wout, seq_len):
    bsz, lp, d = x.shape
    e_dim = wout.shape[0]
    tm, te = ROW_TILE, E_TILE
    n_m, n_e = lp // tm, e_dim // te
    st_row = ((seq_len - 2) % tm) // SUBLANES * SUBLANES
    kern = functools.partial(_conv_prompt_kernel, n_e=n_e, st_row=st_row)
    x_new, st = pl.pallas_call(
        kern,
        grid=(bsz, n_m, n_e),
        in_specs=[
            pl.BlockSpec((None, tm, d), lambda b, i, e: (b, i, 0)),
            pl.BlockSpec((1, d), lambda b, i, e: (0, 0)),
            pl.BlockSpec((4, d, te), lambda b, i, e: (0, 0, e)),
            pl.BlockSpec((CONV_W, te), lambda b, i, e: (0, e)),
            pl.BlockSpec((te, d), lambda b, i, e: (e, 0)),
        ],
        out_specs=[
            pl.BlockSpec((None, tm, d), lambda b, i, e: (b, i, 0)),
            pl.BlockSpec((None, None, SUBLANES, te), lambda b, i, e: (b, i, 0, e)),
        ],
        out_shape=[
            jax.ShapeDtypeStruct((bsz, lp, d), _F32),
            jax.ShapeDtypeStruct((bsz, n_m, SUBLANES, e_dim), _F32),
        ],
        scratch_shapes=[
            pltpu.VMEM((tm, d), _BF16),
            pltpu.VMEM((tm, d), _F32),
            pltpu.VMEM((n_e, SUBLANES, te), _F32),
        ],
        compiler_params=_params(3),
        name="conv_prompt",
    )(x, g, win4, cw, wout)
    i_st = (seq_len - 2) // tm
    r = (seq_len - 2) % SUBLANES
    return x_new, st[:, i_st, r:r + 2, :]


def _conv_sample_kernel(x_ref, g_ref, win_ref, cw_ref, wout_ref, p0_ref, p1_ref,
                        xo_ref, u_ref, h_s, acc_s, *, n_e):
    e = pl.program_id(0)

    @pl.when(e == 0)
    def _():
        h_s[...] = _rms(x_ref[...], g_ref[...]).astype(_BF16)
        acc_s[...] = jnp.zeros_like(acc_s)

    h = h_s[...]
    b = _dot(h, win_ref[0])
    u = _dot(h, win_ref[1]) * _dot(h, win_ref[2])
    z = _dot(h, win_ref[3])
    u_ref[...] = u
    conv = cw_ref[0:1, :] * p0_ref[...] + cw_ref[1:2, :] * p1_ref[...] + cw_ref[2:3, :] * u
    y = (b * conv * _silu(z)).astype(_BF16)
    acc_s[...] += _dot(y, wout_ref[...])

    @pl.when(e == n_e - 1)
    def _():
        xo_ref[...] = x_ref[...] + acc_s[...]


def _conv_sample(x, g, win4, cw, wout, p0, p1):
    n, d = x.shape
    e_dim = wout.shape[0]
    te = E_TILE
    n_e = e_dim // te
    kern = functools.partial(_conv_sample_kernel, n_e=n_e)
    return pl.pallas_call(
        kern,
        grid=(n_e,),
        in_specs=[
            pl.BlockSpec((n, d), lambda e: (0, 0)),
            pl.BlockSpec((1, d), lambda e: (0, 0)),
            pl.BlockSpec((4, d, te), lambda e: (0, 0, e)),
            pl.BlockSpec((CONV_W, te), lambda e: (0, e)),
            pl.BlockSpec((te, d), lambda e: (e, 0)),
            pl.BlockSpec((n, te), lambda e: (0, e)),
            pl.BlockSpec((n, te), lambda e: (0, e)),
        ],
        out_specs=[
            pl.BlockSpec((n, d), lambda e: (0, 0)),
            pl.BlockSpec((n, te), lambda e: (0, e)),
        ],
        out_shape=[
            jax.ShapeDtypeStruct((n, d), _F32),
            jax.ShapeDtypeStruct((n, e_dim), _F32),
        ],
        scratch_shapes=[pltpu.VMEM((n, d), _BF16), pltpu.VMEM((n, d), _F32)],
        compiler_params=_params(1),
        name="conv_sample",
    )(x, g, win4, cw, wout, p0, p1)


def _attn_in_kernel(x_ref, g_ref, w_ref, kf_ref, vf_ref, qb_ref, kb_ref, vb_ref, z_ref):
    h = _rms(x_ref[...], g_ref[...]).astype(_BF16)
    q = _dot(h, w_ref[0])
    qb_ref[...] = (q * (HEAD_DIM ** -0.5 * LOG2E)).astype(_BF16)
    k = _dot(h, w_ref[1])
    kf_ref[...] = k
    kb_ref[...] = k.astype(_BF16)
    v = _dot(h, w_ref[2])
    vf_ref[...] = v
    vb_ref[...] = v.astype(_BF16)
    z_ref[...] = _dot(h, w_ref[3])


def _attn_in_prompt(x, g, w4, seq_len):
    bsz, lp, d = x.shape
    tm = IN_TILE
    row = lambda b, i: (b, i, 0)
    return pl.pallas_call(
        _attn_in_kernel,
        grid=(bsz, lp // tm),
        in_specs=[
            pl.BlockSpec((None, tm, d), row),
            pl.BlockSpec((1, d), lambda b, i: (0, 0)),
            pl.BlockSpec((4, d, d), lambda b, i: (0, 0, 0)),
        ],
        out_specs=[pl.BlockSpec((None, tm, d), row)] * 6,
        out_shape=[
            jax.ShapeDtypeStruct((bsz, seq_len, d), _F32),
            jax.ShapeDtypeStruct((bsz, seq_len, d), _F32),
            jax.ShapeDtypeStruct((bsz, lp, d), _BF16),
            jax.ShapeDtypeStruct((bsz, lp, d), _BF16),
            jax.ShapeDtypeStruct((bsz, lp, d), _BF16),
            jax.ShapeDtypeStruct((bsz, lp, d), _F32),
        ],
        compiler_params=_params(2),
        name="attn_in_prompt",
    )(x, g, w4)


def _attn_in_sample_kernel(x_ref, g_ref, w_ref, q_ref, k_ref, v_ref, z_ref):
    h = _rms(x_ref[...], g_ref[...]).astype(_BF16)
    q_ref[...] = _dot(h, w_ref[0]) * (HEAD_DIM ** -0.5 * LOG2E)
    k_ref[...] = _dot(h, w_ref[1])
    v_ref[...] = _dot(h, w_ref[2])
    z_ref[...] = _dot(h, w_ref[3])


def _attn_in_sample(x, g, w4):
    n, d = x.shape
    full = pl.BlockSpec((n, d), lambda i: (0, 0))
    return pl.pallas_call(
        _attn_in_sample_kernel,
        grid=(1,),
        in_specs=[full, pl.BlockSpec((1, d), lambda i: (0, 0)),
                  pl.BlockSpec((4, d, d), lambda i: (0, 0, 0))],
        out_specs=[full] * 4,
        out_shape=[jax.ShapeDtypeStruct((n, d), _F32)] * 4,
        compiler_params=_params(1),
        name="attn_in_sample",
    )(x, g, w4)


def _attn_out_kernel(o_ref, z_ref, x_ref, w_ref, xo_ref):
    y = (o_ref[...] * _silu(z_ref[...])).astype(_BF16)
    xo_ref[...] = x_ref[...] + _dot(y, w_ref[...])


def _attn_out(o, z, x, w, tm):
    rows, d = x.shape
    row = pl.BlockSpec((tm, d), lambda i: (i, 0))
    return pl.pallas_call(
        _attn_out_kernel,
        grid=(rows // tm,),
        in_specs=[row, row, row, pl.BlockSpec((d, d), lambda i: (0, 0))],
        out_specs=row,
        out_shape=jax.ShapeDtypeStruct((rows, d), _F32),
        compiler_params=_params(1),
        name="attn_out",
    )(o, z, x, w)


def _final_norm_kernel(x_ref, g_ref, y_ref):
    y_ref[...] = _rms(x_ref[...], g_ref[...])


def _final_norm(x, g, tm):
    rows, d = x.shape
    row = pl.BlockSpec((tm, d), lambda i: (i, 0))
    return pl.pallas_call(
        _final_norm_kernel,
        grid=(rows // tm,),
        in_specs=[row, pl.BlockSpec((1, d), lambda i: (0, 0))],
        out_specs=row,
        out_shape=jax.ShapeDtypeStruct((rows, d), _F32),
        compiler_params=_params(1),
        name="final_norm",
    )(x, g)


def _sb_prompt_kernel(kp_ref, qt_ref, vt_ref, u_ref, o_ref):
    qi = pl.program_id(2)
    n_heads = qt_ref.shape[0]
    tk = u_ref.shape[0]
    tq = qt_ref.shape[2]
    heads = range(n_heads)
    steps = range(TILES_PER_STEP)

    def group(states, j0, diagonal):
        js = [j0 - k for k in steps]
        live = [(j >= 0).astype(_F32) for j in js]
        jc = [jnp.maximum(j, 0) for j in js]
        z = [[_dot(kp_ref[hh, jc[k]], qt_ref[hh]) for k in steps] for hh in heads]
        sp = [[_softplus2(z[hh][k]) for k in steps] for hh in heads]
        if diagonal:
            visible = (lax.broadcasted_iota(jnp.int32, (tk, tq), 0)
                       < lax.broadcasted_iota(jnp.int32, (tk, tq), 1))
            for hh in heads:
                sp[hh][0] = jnp.where(visible, sp[hh][0], 0.0)
        c = [[_dot(u_ref[...], sp[hh][k].astype(_BF16)) for k in steps] for hh in heads]
        p = [[jnp.exp2(z[hh][k] - c[hh][k]) for k in steps] for hh in heads]
        if diagonal:
            for hh in heads:
                p[hh][0] = jnp.where(visible, p[hh][0], 0.0)
        pv = [[_dot(vt_ref[hh, jc[k]], p[hh][k].astype(_BF16)) for k in steps] for hh in heads]
        out = []
        for hh in heads:
            acc, r = states[hh]
            for k in steps:
                acc = acc + (live[k] * jnp.exp2(-r)) * pv[hh][k]
                r = r + live[k] * c[hh][k][0:1, :]
            out.append((acc, r))
        return tuple(out)

    zero = tuple((jnp.zeros((HEAD_DIM, tq), _F32), jnp.zeros((1, tq), _F32)) for _ in heads)
    states = group(zero, qi, True)
    n_more = qi // TILES_PER_STEP

    def body(it, states):
        return group(states, qi - (it + 1) * TILES_PER_STEP, False)

    states = lax.fori_loop(0, n_more, body, states)
    o_ref[...] = jnp.concatenate([st[0] for st in states], axis=0).T


def _sb_prompt(kp, qt, vt, tri):
    bsz, n_h, n_k, tk, qk = kp.shape
    lp = n_k * tk
    tq = tk
    return pl.pallas_call(
        _sb_prompt_kernel,
        grid=(bsz, n_h // 2, lp // tq),
        in_specs=[
            pl.BlockSpec((None, 2, n_k, tk, qk), lambda b, h, i: (b, h, 0, 0, 0)),
            pl.BlockSpec((None, 2, qk, tq), lambda b, h, i: (b, h, 0, i)),
            pl.BlockSpec((None, 2, n_k, HEAD_DIM, tk), lambda b, h, i: (b, h, 0, 0, 0)),
            pl.BlockSpec((tk, tk), lambda b, h, i: (0, 0)),
        ],
        out_specs=pl.BlockSpec((None, tq, 2 * HEAD_DIM), lambda b, h, i: (b, i, h)),
        out_shape=jax.ShapeDtypeStruct((bsz, lp, n_h * HEAD_DIM), _F32),
        compiler_params=_params(3),
        name="sb_prompt",
    )(kp, qt, vt, tri)


def _split3(x):
    hi = x.astype(_BF16)
    mid = (x - hi.astype(_F32)).astype(_BF16)
    lo = (x - hi.astype(_F32) - mid.astype(_F32)).astype(_BF16)
    return hi, mid, lo


def _prompt_attention_operands(qb, kb, vb, bias):
    bsz, lp, _ = qb.shape
    tk = ATT_TILE
    n_k = lp // tk
    pad = QK_DIM - HEAD_DIM - 3
    k4 = kb.reshape(bsz, n_k, tk, N_HEADS, HEAD_DIM).transpose(0, 3, 1, 2, 4)
    kp = jnp.concatenate([
        k4,
        jnp.ones(k4.shape[:-1] + (3,), _BF16),
        jnp.zeros(k4.shape[:-1] + (pad,), _BF16)], axis=-1)
    q4 = qb.reshape(bsz, lp, N_HEADS, HEAD_DIM).transpose(0, 2, 3, 1)
    parts = jnp.stack(_split3(bias.astype(_F32) * LOG2E), axis=-1)
    brow = jnp.broadcast_to(parts[None, :, :, None], (bsz, N_HEADS, 3, lp))
    qt = jnp.concatenate([q4, brow, jnp.zeros((bsz, N_HEADS, pad, lp), _BF16)], axis=2)
    vt = vb.reshape(bsz, n_k, tk, N_HEADS, HEAD_DIM).transpose(0, 3, 1, 4, 2)
    return kp, qt, vt


def _tri(n):
    r = lax.broadcasted_iota(jnp.int32, (n, n), 0)
    c = lax.broadcasted_iota(jnp.int32, (n, n), 1)
    return (c >= r).astype(_BF16)


def _sb_sample_kernel(pt_ref, *refs, n_pages):
    del pt_ref
    k_refs = refs[:n_pages]
    v_refs = refs[n_pages:2 * n_pages]
    q_ref, bias_ref, low_ref, o_ref, acc_s = refs[2 * n_pages:]
    hd, page = k_refs[0].shape
    qcol = jnp.transpose(jnp.broadcast_to(q_ref[...], (page, hd)))
    bias = bias_ref[...]
    low = low_ref[...]
    r = jnp.zeros((N_HEADS, page), _F32)
    acc_s[...] = jnp.zeros_like(acc_s)
    for p in reversed(range(n_pages)):
        z = (k_refs[p][...] * qcol).reshape(N_HEADS, HEAD_DIM, page).sum(axis=1) + bias
        sp = _softplus2(z)
        sp_hi = sp.astype(_BF16)
        sp_lo = (sp - sp_hi.astype(_F32)).astype(_BF16)
        c = _dot(sp_hi, low) + _dot(sp_lo, low)
        w = jnp.exp2(z - c - r)
        r = r + jnp.broadcast_to(c[:, 0:1], r.shape)
        wx = jnp.broadcast_to(w[:, None, :], (N_HEADS, HEAD_DIM, page)).reshape(hd, page)
        acc_s[...] += v_refs[p][...] * wx
    o_ref[...] = jnp.sum(jnp.transpose(acc_s[...]), axis=0, keepdims=True)


def _sb_sample(page_table, cache_kt, cache_vt, layer, q2, bias2, low):
    n, n_pages = page_table.shape
    _, _, hd, page = cache_kt.shape

    def page_spec(p):
        return pl.BlockSpec((None, None, hd, page), lambda b, pt, p=p: (layer, pt[b, p], 0, 0))

    in_specs = ([page_spec(p) for p in range(n_pages)] * 2 + [
        pl.BlockSpec((None, 1, hd), lambda b, pt: (b, 0, 0)),
        pl.BlockSpec((N_HEADS, page), lambda b, pt: (0, 0)),
        pl.BlockSpec((page, page), lambda b, pt: (0, 0)),
    ])
    out = pl.pallas_call(
        functools.partial(_sb_sample_kernel, n_pages=n_pages),
        grid_spec=pltpu.PrefetchScalarGridSpec(
            num_scalar_prefetch=1,
            grid=(n,),
            in_specs=in_specs,
            out_specs=pl.BlockSpec((None, 1, hd), lambda b, pt: (b, 0, 0)),
            scratch_shapes=[pltpu.VMEM((hd, page), _F32)],
        ),
        out_shape=jax.ShapeDtypeStruct((n, 1, hd), _F32),
        compiler_params=_params(1),
        name="sb_sample",
    )(page_table, *([cache_kt] * n_pages), *([cache_vt] * n_pages), q2, bias2, low)
    return out.reshape(n, hd)


def kernel(x_prompt, x_sample, state_conv, cache_k, cache_v, page_table, meta_tokens, norm_g, final_norm_g,
           w_in_conv, conv_w, w_out_conv, w_in_attn, w_out_attn, sb_bias):
    bsz, seq, d = x_prompt.shape
    n_s = x_sample.shape[0]
    seq_len = N_META + seq
    lp = -(-seq_len // ROW_TILE) * ROW_TILE
    assert lp - seq_len < IN_TILE, (lp, seq_len)
    depth = norm_g.shape[0]

    meta = jnp.broadcast_to(meta_tokens[None].astype(x_prompt.dtype), (bsz, N_META, d))
    xp = jnp.concatenate([meta, x_prompt, jnp.zeros((bsz, lp - seq_len, d), x_prompt.dtype)], axis=1)
    xs = x_sample.reshape(n_s, d)

    n_phys, page = cache_k.shape[1], cache_k.shape[2]
    ckt = cache_k.transpose(0, 1, 3, 4, 2).reshape(cache_k.shape[0], n_phys, N_HEADS * HEAD_DIM, page)
    cvt = cache_v.transpose(0, 1, 3, 4, 2).reshape(cache_v.shape[0], n_phys, N_HEADS * HEAD_DIM, page)
    tri_p = _tri(ATT_TILE)
    low_s = _tri(page).T

    conv_p, conv_s, kp_l, vp_l, ks_l, vs_l = [], [], [], [], [], []
    for i in range(depth):
        g = norm_g[i].reshape(1, d)
        if i % 2 == 0:
            c = i // 2
            win4 = w_in_conv[c].reshape(d, 4, E_CONV).transpose(1, 0, 2).astype(_BF16)
            wout = w_out_conv[c].astype(_BF16)
            xp, stp = _conv_prompt(xp, g, win4, conv_w[c], wout, seq_len)
            xs, u_s = _conv_sample(xs, g, win4, conv_w[c], wout, state_conv[c, :, 0, :], state_conv[c, :, 1, :])
            conv_p.append(stp)
            conv_s.append(jnp.stack([state_conv[c, :, 1, :], u_s], axis=1))
        else:
            a = i // 2
            w4 = w_in_attn[a].reshape(d, 4, d).transpose(1, 0, 2).astype(_BF16)
            wout = w_out_attn[a].astype(_BF16)
            kf, vf, qb, kb, vb, zp = _attn_in_prompt(xp, g, w4, seq_len)
            kp, qt, vt = _prompt_attention_operands(qb, kb, vb, sb_bias[a])
            op = _sb_prompt(kp, qt, vt, tri_p)
            xp = _attn_out(op.reshape(bsz * lp, d), zp.reshape(bsz * lp, d), xp.reshape(bsz * lp, d),
                           wout, ROW_TILE).reshape(bsz, lp, d)
            kp_l.append(kf.reshape(bsz, seq_len, N_HEADS, HEAD_DIM))
            vp_l.append(vf.reshape(bsz, seq_len, N_HEADS, HEAD_DIM))
            q2, kn, vn, zs = _attn_in_sample(xs, g, w4)
            bias2 = jnp.broadcast_to((sb_bias[a].astype(_F32) * LOG2E)[:, None], (N_HEADS, page))
            os_ = _sb_sample(page_table, ckt, cvt, a, q2.reshape(n_s, 1, d), bias2, low_s)
            xs = _attn_out(os_, zs, xs, wout, n_s)
            ks_l.append(kn.reshape(n_s, 1, N_HEADS, HEAD_DIM))
            vs_l.append(vn.reshape(n_s, 1, N_HEADS, HEAD_DIM))

    gf = final_norm_g.reshape(1, d)
    y_prompt = _final_norm(xp.reshape(bsz * lp, d), gf, ROW_TILE).reshape(bsz, lp, d)[:, N_META:seq_len]
    y_sample = _final_norm(xs, gf, n_s).reshape(n_s, 1, d)
    return (y_prompt, y_sample, jnp.stack(conv_p), jnp.stack(conv_s),
            jnp.stack(kp_l), jnp.stack(vp_l), jnp.stack(ks_l), jnp.stack(vs_l))
```

```python
import functools
import math

import jax
import jax.numpy as jnp
from jax import lax
from jax.experimental import pallas as pl
from jax.experimental.pallas import tpu as pltpu

D_MODEL = 1024
N_HEADS = 16
HEAD_DIM = 64
E_CONV = 2 * D_MODEL
CONV_W = 3
N_META = 16
EPS = 1e-6
LOG2E = 1.4426950408889634

SUBLANES = 8
LANES = 128
MXU_DIM = 256

ATT_TILE = MXU_DIM
ROW_TILE = 3 * ATT_TILE
IN_TILE = ATT_TILE
E_TILE = 512
QK_DIM = 2 * HEAD_DIM
TILES_PER_STEP = 4
VMEM_LIMIT = 56 * 1024 * 1024

_BF16 = jnp.bfloat16
_F32 = jnp.float32


def _params(n_axes):
    return pltpu.CompilerParams(
        dimension_semantics=("arbitrary",) * n_axes, vmem_limit_bytes=VMEM_LIMIT)


def _dot(a, b):
    return jnp.dot(a, b, preferred_element_type=_F32)


def _rms(x, g):
    ms = jnp.mean(x * x, axis=-1, keepdims=True)
    return x * lax.rsqrt(ms + EPS) * g


def _silu(z):
    return z * jax.nn.sigmoid(z)


def _softplus2(z):
    return jnp.maximum(z, 0.0) + jnp.log2(1.0 + jnp.exp2(-jnp.abs(z)))


def _conv_prompt_kernel(x_ref, g_ref, win_ref, cw_ref, wout_ref, xo_ref, st_ref,
                        h_s, acc_s, carry_s, *, n_e, st_row):
    i = pl.program_id(1)
    e = pl.program_id(2)

    @pl.when(e == 0)
    def _():
        h_s[...] = _rms(x_ref[...], g_ref[...]).astype(_BF16)
        acc_s[...] = jnp.zeros_like(acc_s)

    @pl.when(i == 0)
    def _():
        carry_s[e] = jnp.zeros(carry_s.shape[1:], _F32)

    h = h_s[...]
    b = _dot(h, win_ref[0])
    u = _dot(h, win_ref[1]) * _dot(h, win_ref[2])
    z = _dot(h, win_ref[3])

    prev = carry_s[e]
    row = lax.broadcasted_iota(jnp.int32, u.shape, 0)
    u1 = jnp.where(row == 0, prev[7:8, :], pltpu.roll(u, 1, 0))
    u2 = jnp.where(row == 0, prev[6:7, :],
                   jnp.where(row == 1, prev[7:8, :], pltpu.roll(u, 2, 0)))
    tm = u.shape[0]
    carry_s[e] = u[tm - SUBLANES:, :]
    st_ref[...] = u[st_row:st_row + SUBLANES, :]

    conv = cw_ref[0:1, :] * u2 + cw_ref[1:2, :] * u1 + cw_ref[2:3, :] * u
    y = (b * conv * _silu(z)).astype(_BF16)
    acc_s[...] += _dot(y, wout_ref[...])

    @pl.when(e == n_e - 1)
    def _():
        xo_ref[...] = x_ref[...] + acc_s[...]


def _conv_prompt(x, g, win4, cw, wout, seq_len):
    bsz, lp, d = x.shape
    e_dim = wout.shape[0]
    tm, te = ROW_TILE, E_TILE
    n_m, n_e = lp // tm, e_dim // te
    st_row = ((seq_len - 2) % tm) // SUBLANES * SUBLANES
    kern = functools.partial(_conv_prompt_kernel, n_e=n_e, st_row=st_row)
    x_new, st = pl.pallas_call(
        kern,
        grid=(bsz, n_m, n_e),
        in_specs=[
            pl.BlockSpec((None, tm, d), lambda b, i, e: (b, i, 0)),
            pl.BlockSpec((1, d), lambda b, i, e: (0, 0)),
            pl.BlockSpec((4, d, te), lambda b, i, e: (0, 0, e)),
            pl.BlockSpec((CONV_W, te), lambda b, i, e: (0, e)),
            pl.BlockSpec((te, d), lambda b, i, e: (e, 0)),
        ],
        out_specs=[
            pl.BlockSpec((None, tm, d), lambda b, i, e: (b, i, 0)),
            pl.BlockSpec((None, None, SUBLANES, te), lambda b, i, e: (b, i, 0, e)),
        ],
        out_shape=[
            jax.ShapeDtypeStruct((bsz, lp, d), _F32),
            jax.ShapeDtypeStruct((bsz, n_m, SUBLANES, e_dim), _F32),
        ],
        scratch_shapes=[
            pltpu.VMEM((tm, d), _BF16),
            pltpu.VMEM((tm, d), _F32),
            pltpu.VMEM((n_e, SUBLANES, te), _F32),
        ],
        compiler_params=_params(3),
        name="conv_prompt",
    )(x, g, win4, cw, wout)
    i_st = (seq_len - 2) // tm
    r = (seq_len - 2) % SUBLANES
    return x_new, st[:, i_st, r:r + 2, :]


def _conv_sample_kernel(x_ref, g_ref, win_ref, cw_ref, wout_ref, p0_ref, p1_ref,
                        xo_ref, u_ref, h_s, acc_s, *, n_e):
    e = pl.program_id(0)

    @pl.when(e == 0)
    def _():
        h_s[...] = _rms(x_ref[...], g_ref[...]).astype(_BF16)
        acc_s[...] = jnp.zeros_like(acc_s)

    h = h_s[...]
    b = _dot(h, win_ref[0])
    u = _dot(h, win_ref[1]) * _dot(h, win_ref[2])
    z = _dot(h, win_ref[3])
    u_ref[...] = u
    conv = cw_ref[0:1, :] * p0_ref[...] + cw_ref[1:2, :] * p1_ref[...] + cw_ref[2:3, :] * u
    y = (b * conv * _silu(z)).astype(_BF16)
    acc_s[...] += _dot(y, wout_ref[...])

    @pl.when(e == n_e - 1)
    def _():
        xo_ref[...] = x_ref[...] + acc_s[...]


def _conv_sample(x, g, win4, cw, wout, p0, p1):
    n, d = x.shape
    e_dim = wout.shape[0]
    te = E_TILE
    n_e = e_dim // te
    kern = functools.partial(_conv_sample_kernel, n_e=n_e)
    return pl.pallas_call(
        kern,
        grid=(n_e,),
        in_specs=[
            pl.BlockSpec((n, d), lambda e: (0, 0)),
            pl.BlockSpec((1, d), lambda e: (0, 0)),
            pl.BlockSpec((4, d, te), lambda e: (0, 0, e)),
            pl.BlockSpec((CONV_W, te), lambda e: (0, e)),
            pl.BlockSpec((te, d), lambda e: (e, 0)),
            pl.BlockSpec((n, te), lambda e: (0, e)),
            pl.BlockSpec((n, te), lambda e: (0, e)),
        ],
        out_specs=[
            pl.BlockSpec((n, d), lambda e: (0, 0)),
            pl.BlockSpec((n, te), lambda e: (0, e)),
        ],
        out_shape=[
            jax.ShapeDtypeStruct((n, d), _F32),
            jax.ShapeDtypeStruct((n, e_dim), _F32),
        ],
        scratch_shapes=[pltpu.VMEM((n, d), _BF16), pltpu.VMEM((n, d), _F32)],
        compiler_params=_params(1),
        name="conv_sample",
    )(x, g, win4, cw, wout, p0, p1)


def _attn_in_kernel(x_ref, g_ref, w_ref, btab_ref, kf_ref, vf_ref, z_ref, kp_ref, qt_ref, vt_ref):
    h = _rms(x_ref[...], g_ref[...]).astype(_BF16)
    q = _dot(h, w_ref[0]) * (HEAD_DIM ** -0.5 * LOG2E)
    k = _dot(h, w_ref[1])
    kf_ref[...] = k
    v = _dot(h, w_ref[2])
    vf_ref[...] = v
    z_ref[...] = _dot(h, w_ref[3])

    tm = k.shape[0]
    pair = 2 * HEAD_DIM
    lane = lax.broadcasted_iota(jnp.int32, (tm, pair), 1)
    slot = lax.broadcasted_iota(jnp.int32, (pair, tm), 0)
    ones_even = ((lane >= HEAD_DIM) & (lane < HEAD_DIM + 3)).astype(_F32)
    ones_odd = (lane < 3).astype(_F32)
    for hp in range(kp_ref.shape[0]):
        cols = slice(hp * pair, (hp + 1) * pair)
        kpair = k[:, cols]
        kp_ref[hp, 0] = jnp.where(lane < HEAD_DIM, kpair, ones_even).astype(_BF16)
        kp_ref[hp, 1] = jnp.where(lane >= HEAD_DIM, kpair, ones_odd).astype(_BF16)
        qpair_t = q[:, cols].T
        qt_ref[hp, 0] = jnp.where(slot < HEAD_DIM, qpair_t, btab_ref[2 * hp]).astype(_BF16)
        qt_ref[hp, 1] = jnp.where(slot >= HEAD_DIM, qpair_t, btab_ref[2 * hp + 1]).astype(_BF16)
        vt_ref[hp] = v[:, cols].T.astype(_BF16)


def _attn_in_prompt(x, g, w4, btab, seq_len):
    bsz, lp, d = x.shape
    tm = IN_TILE
    n_t = lp // tm
    n_pairs = N_HEADS // 2
    pair = 2 * HEAD_DIM
    row = lambda b, i: (b, i, 0)
    return pl.pallas_call(
        _attn_in_kernel,
        grid=(bsz, n_t),
        in_specs=[
            pl.BlockSpec((None, tm, d), row),
            pl.BlockSpec((1, d), lambda b, i: (0, 0)),
            pl.BlockSpec((4, d, d), lambda b, i: (0, 0, 0)),
            pl.BlockSpec((N_HEADS, pair, tm), lambda b, i: (0, 0, 0)),
        ],
        out_specs=[
            pl.BlockSpec((None, tm, d), row),
            pl.BlockSpec((None, tm, d), row),
            pl.BlockSpec((None, tm, d), row),
            pl.BlockSpec((None, n_pairs, None, 2, tm, pair), lambda b, i: (b, 0, i, 0, 0, 0)),
            pl.BlockSpec((None, n_pairs, None, 2, pair, tm), lambda b, i: (b, 0, i, 0, 0, 0)),
            pl.BlockSpec((None, n_pairs, None, pair, tm), lambda b, i: (b, 0, i, 0, 0)),
        ],
        out_shape=[
            jax.ShapeDtypeStruct((bsz, seq_len, d), _F32),
            jax.ShapeDtypeStruct((bsz, seq_len, d), _F32),
            jax.ShapeDtypeStruct((bsz, lp, d), _F32),
            jax.ShapeDtypeStruct((bsz, n_pairs, n_t, 2, tm, pair), _BF16),
            jax.ShapeDtypeStruct((bsz, n_pairs, n_t, 2, pair, tm), _BF16),
            jax.ShapeDtypeStruct((bsz, n_pairs, n_t, pair, tm), _BF16),
        ],
        compiler_params=_params(2),
        name="attn_in_prompt",
    )(x, g, w4, btab)


def _bias_table(bias, tq):
    parts = jnp.stack([p.astype(_F32) for p in _split3(bias.astype(_F32) * LOG2E)], axis=-1)
    first = jnp.where(jnp.arange(N_HEADS) % 2 == 0, HEAD_DIM, 0)
    slot = jnp.arange(2 * HEAD_DIM)[None, :] - first[:, None]
    tab = jnp.where((slot >= 0) & (slot < 3), jnp.take_along_axis(parts, jnp.clip(slot, 0, 2), axis=1), 0.0)
    return jnp.broadcast_to(tab[:, :, None], (N_HEADS, 2 * HEAD_DIM, tq))


def _attn_in_sample_kernel(x_ref, g_ref, w_ref, q_ref, k_ref, v_ref, z_ref):
    h = _rms(x_ref[...], g_ref[...]).astype(_BF16)
    q_ref[...] = _dot(h, w_ref[0]) * (HEAD_DIM ** -0.5 * LOG2E)
    k_ref[...] = _dot(h, w_ref[1])
    v_ref[...] = _dot(h, w_ref[2])
    z_ref[...] = _dot(h, w_ref[3])


def _attn_in_sample(x, g, w4):
    n, d = x.shape
    full = pl.BlockSpec((n, d), lambda i: (0, 0))
    return pl.pallas_call(
        _attn_in_sample_kernel,
        grid=(1,),
        in_specs=[full, pl.BlockSpec((1, d), lambda i: (0, 0)),
                  pl.BlockSpec((4, d, d), lambda i: (0, 0, 0))],
        out_specs=[full] * 4,
        out_shape=[jax.ShapeDtypeStruct((n, d), _F32)] * 4,
        compiler_params=_params(1),
        name="attn_in_sample",
    )(x, g, w4)


def _attn_out_kernel(o_ref, z_ref, x_ref, w_ref, xo_ref):
    y = (o_ref[...] * _silu(z_ref[...])).astype(_BF16)
    xo_ref[...] = x_ref[...] + _dot(y, w_ref[...])


def _attn_out(o, z, x, w, tm):
    rows, d = x.shape
    row = pl.BlockSpec((tm, d), lambda i: (i, 0))
    return pl.pallas_call(
        _attn_out_kernel,
        grid=(rows // tm,),
        in_specs=[row, row, row, pl.BlockSpec((d, d), lambda i: (0, 0))],
        out_specs=row,
        out_shape=jax.ShapeDtypeStruct((rows, d), _F32),
        compiler_params=_params(1),
        name="attn_out",
    )(o, z, x, w)


def _final_norm_kernel(x_ref, g_ref, y_ref):
    y_ref[...] = _rms(x_ref[...], g_ref[...])


def _final_norm(x, g, tm):
    rows, d = x.shape
    row = pl.BlockSpec((tm, d), lambda i: (i, 0))
    return pl.pallas_call(
        _final_norm_kernel,
        grid=(rows // tm,),
        in_specs=[row, pl.BlockSpec((1, d), lambda i: (0, 0))],
        out_specs=row,
        out_shape=jax.ShapeDtypeStruct((rows, d), _F32),
        compiler_params=_params(1),
        name="final_norm",
    )(x, g)


def _sb_prompt_kernel(kp_ref, qt_ref, vt_ref, u_ref, o_ref, za_ref, zb_ref, p_ref):
    qi = pl.program_id(2)
    n_heads = qt_ref.shape[0]
    tk = u_ref.shape[0]
    tq = qt_ref.shape[2]
    heads = range(n_heads)
    steps = range(TILES_PER_STEP)

    def scores(z_ref, j0):
        for hh in heads:
            for k in steps:
                z_ref[hh * TILES_PER_STEP + k] = _dot(kp_ref[jnp.maximum(j0 - k, 0), hh], qt_ref[hh])

    def fold(states, pending, slot):
        j0, cs = pending
        out = []
        pv = [[_dot(vt_ref[jnp.maximum(j0 - k, 0), pl.ds(hh * HEAD_DIM, HEAD_DIM), :],
                    p_ref[slot, hh * TILES_PER_STEP + k])
               for k in steps] for hh in heads]
        for hh in heads:
            acc, r = states[hh]
            for k in steps:
                live = (j0 - k >= 0).astype(_F32)
                acc = acc + (live * jnp.exp2(-r)) * pv[hh][k]
                r = r + live * cs[hh][k]
            out.append((acc, r))
        return tuple(out)

    def group(carry, z_ref, nxt_ref, slot, j0, first):
        states, pending = carry
        if not first:
            states = fold(states, pending, 1 - slot)
        z = [[z_ref[hh * TILES_PER_STEP + k] for k in steps] for hh in heads]
        sp = [[_softplus2(z[hh][k]) for k in steps] for hh in heads]
        if first:
            visible = (lax.broadcasted_iota(jnp.int32, (tk, tq), 0)
                       < lax.broadcasted_iota(jnp.int32, (tk, tq), 1))
            for hh in heads:
                sp[hh][0] = jnp.where(visible, sp[hh][0], 0.0)
        c = [[None for k in steps] for hh in heads]
        for hh in heads:
            for k in steps:
                c[hh][k] = _dot(u_ref[...], sp[hh][k].astype(_BF16))
                nxt_ref[hh * TILES_PER_STEP + k] = _dot(
                    kp_ref[jnp.maximum(j0 - TILES_PER_STEP - k, 0), hh], qt_ref[hh])
        for hh in heads:
            for k in steps:
                p = jnp.exp2(z_ref[hh * TILES_PER_STEP + k] - c[hh][k])
                if first and k == 0:
                    p = jnp.where(visible, p, 0.0)
                p_ref[slot, hh * TILES_PER_STEP + k] = p.astype(_BF16)
        return states, (j0, tuple(tuple(c[hh][k][0:1, :] for k in steps) for hh in heads))

    zero = tuple((jnp.zeros((HEAD_DIM, tq), _F32), jnp.zeros((1, tq), _F32)) for _ in heads)
    nothing = (qi, tuple(tuple(jnp.zeros((1, tq), _F32) for _ in steps) for _ in heads))
    scores(za_ref, qi)
    carry = group((zero, nothing), za_ref, zb_ref, 0, qi, True)
    n_more = qi // TILES_PER_STEP

    def body(it, carry):
        j0 = qi - (2 * it + 1) * TILES_PER_STEP
        carry = group(carry, zb_ref, za_ref, 1, j0, False)
        j1 = j0 - TILES_PER_STEP
        return lax.cond(j1 >= 0, lambda c: group(c, za_ref, zb_ref, 0, j1, False), lambda c: c, carry)

    states, pending = lax.fori_loop(0, (n_more + 1) // 2, body, carry)
    states = fold(states, pending, lax.rem(n_more, 2))
    o_ref[...] = jnp.concatenate([st[0] for st in states], axis=0).T


def _sb_prompt(kp, qt, vt, tri):
    bsz, n_pairs, n_k, _, tk, qk = kp.shape
    lp = n_k * tk
    tq = tk
    return pl.pallas_call(
        _sb_prompt_kernel,
        grid=(bsz, n_pairs, lp // tq),
        in_specs=[
            pl.BlockSpec((None, None, n_k, 2, tk, qk), lambda b, h, i: (b, h, 0, 0, 0, 0)),
            pl.BlockSpec((None, None, None, 2, qk, tq), lambda b, h, i: (b, h, i, 0, 0, 0)),
            pl.BlockSpec((None, None, n_k, 2 * HEAD_DIM, tk), lambda b, h, i: (b, h, 0, 0, 0)),
            pl.BlockSpec((tk, tk), lambda b, h, i: (0, 0)),
        ],
        out_specs=pl.BlockSpec((None, tq, 2 * HEAD_DIM), lambda b, h, i: (b, i, h)),
        out_shape=jax.ShapeDtypeStruct((bsz, lp, 2 * n_pairs * HEAD_DIM), _F32),
        scratch_shapes=[
            pltpu.VMEM((2 * TILES_PER_STEP, tk, tq), _F32),
            pltpu.VMEM((2 * TILES_PER_STEP, tk, tq), _F32),
            pltpu.VMEM((2, 2 * TILES_PER_STEP, tk, tq), _BF16),
        ],
        compiler_params=_params(3),
        name="sb_prompt",
    )(kp, qt, vt, tri)


def _split3(x):
    hi = x.astype(_BF16)
    mid = (x - hi.astype(_F32)).astype(_BF16)
    lo = (x - hi.astype(_F32) - mid.astype(_F32)).astype(_BF16)
    return hi, mid, lo


def _tri(n):
    r = lax.broadcasted_iota(jnp.int32, (n, n), 0)
    c = lax.broadcasted_iota(jnp.int32, (n, n), 1)
    return (c >= r).astype(_BF16)


def _sb_sample_kernel(pt_ref, *refs, n_pages):
    del pt_ref
    k_refs = refs[:n_pages]
    v_refs = refs[n_pages:2 * n_pages]
    q_ref, bias_ref, low_ref, o_ref, acc_s = refs[2 * n_pages:]
    hd, page = k_refs[0].shape
    qcol = jnp.transpose(jnp.broadcast_to(q_ref[...], (page, hd)))
    bias = bias_ref[...]
    low = low_ref[...]
    r = jnp.zeros((N_HEADS, page), _F32)
    acc_s[...] = jnp.zeros_like(acc_s)
    for p in reversed(range(n_pages)):
        z = (k_refs[p][...] * qcol).reshape(N_HEADS, HEAD_DIM, page).sum(axis=1) + bias
        sp = _softplus2(z)
        sp_hi = sp.astype(_BF16)
        sp_lo = (sp - sp_hi.astype(_F32)).astype(_BF16)
        c = _dot(sp_hi, low) + _dot(sp_lo, low)
        w = jnp.exp2(z - c - r)
        r = r + jnp.broadcast_to(c[:, 0:1], r.shape)
        wx = jnp.broadcast_to(w[:, None, :], (N_HEADS, HEAD_DIM, page)).reshape(hd, page)
        acc_s[...] += v_refs[p][...] * wx
    o_ref[...] = jnp.sum(jnp.transpose(acc_s[...]), axis=0, keepdims=True)


def _sb_sample(page_table, cache_kt, cache_vt, layer, q2, bias2, low):
    n, n_pages = page_table.shape
    _, _, hd, page = cache_kt.shape

    def page_spec(p):
        return pl.BlockSpec((None, None, hd, page), lambda b, pt, p=p: (layer, pt[b, p], 0, 0))

    in_specs = ([page_spec(p) for p in range(n_pages)] * 2 + [
        pl.BlockSpec((None, 1, hd), lambda b, pt: (b, 0, 0)),
        pl.BlockSpec((N_HEADS, page), lambda b, pt: (0, 0)),
        pl.BlockSpec((page, page), lambda b, pt: (0, 0)),
    ])
    out = pl.pallas_call(
        functools.partial(_sb_sample_kernel, n_pages=n_pages),
        grid_spec=pltpu.PrefetchScalarGridSpec(
            num_scalar_prefetch=1,
            grid=(n,),
            in_specs=in_specs,
            out_specs=pl.BlockSpec((None, 1, hd), lambda b, pt: (b, 0, 0)),
            scratch_shapes=[pltpu.VMEM((hd, page), _F32)],
        ),
        out_shape=jax.ShapeDtypeStruct((n, 1, hd), _F32),
        compiler_params=_params(1),
        name="sb_sample",
    )(page_table, *([cache_kt] * n_pages), *([cache_vt] * n_pages), q2, bias2, low)
    return out.reshape(n, hd)


def kernel(x_prompt, x_sample, state_conv, cache_k, cache_v, page_table, meta_tokens, norm_g, final_norm_g,
           w_in_conv, conv_w, w_out_conv, w_in_attn, w_out_attn, sb_bias):
    bsz, seq, d = x_prompt.shape
    n_s = x_sample.shape[0]
    seq_len = N_META + seq
    lp = -(-seq_len // ROW_TILE) * ROW_TILE
    assert lp - seq_len < IN_TILE, (lp, seq_len)
    depth = norm_g.shape[0]

    meta = jnp.broadcast_to(meta_tokens[None].astype(x_prompt.dtype), (bsz, N_META, d))
    xp = jnp.concatenate([meta, x_prompt, jnp.zeros((bsz, lp - seq_len, d), x_prompt.dtype)], axis=1)
    xs = x_sample.reshape(n_s, d)

    n_phys, page = cache_k.shape[1], cache_k.shape[2]
    ckt = cache_k.transpose(0, 1, 3, 4, 2).reshape(cache_k.shape[0], n_phys, N_HEADS * HEAD_DIM, page)
    cvt = cache_v.transpose(0, 1, 3, 4, 2).reshape(cache_v.shape[0], n_phys, N_HEADS * HEAD_DIM, page)
    tri_p = _tri(ATT_TILE)
    low_s = _tri(page).T

    conv_p, conv_s, kp_l, vp_l, ks_l, vs_l = [], [], [], [], [], []
    for i in range(depth):
        g = norm_g[i].reshape(1, d)
        if i % 2 == 0:
            c = i // 2
            win4 = w_in_conv[c].reshape(d, 4, E_CONV).transpose(1, 0, 2).astype(_BF16)
            wout = w_out_conv[c].astype(_BF16)
            xp, stp = _conv_prompt(xp, g, win4, conv_w[c], wout, seq_len)
            xs, u_s = _conv_sample(xs, g, win4, conv_w[c], wout, state_conv[c, :, 0, :], state_conv[c, :, 1, :])
            conv_p.append(stp)
            conv_s.append(jnp.stack([state_conv[c, :, 1, :], u_s], axis=1))
        else:
            a = i // 2
            w4 = w_in_attn[a].reshape(d, 4, d).transpose(1, 0, 2).astype(_BF16)
            wout = w_out_attn[a].astype(_BF16)
            kf, vf, zp, kp, qt, vt = _attn_in_prompt(xp, g, w4, _bias_table(sb_bias[a], IN_TILE), seq_len)
            op = _sb_prompt(kp, qt, vt, tri_p)
            xp = _attn_out(op.reshape(bsz * lp, d), zp.reshape(bsz * lp, d), xp.reshape(bsz * lp, d),
                           wout, ROW_TILE).reshape(bsz, lp, d)
            kp_l.append(kf.reshape(bsz, seq_len, N_HEADS, HEAD_DIM))
            vp_l.append(vf.reshape(bsz, seq_len, N_HEADS, HEAD_DIM))
            q2, kn, vn, zs = _attn_in_sample(xs, g, w4)
            bias2 = jnp.broadcast_to((sb_bias[a].astype(_F32) * LOG2E)[:, None], (N_HEADS, page))
            os_ = _sb_sample(page_table, ckt, cvt, a, q2.reshape(n_s, 1, d), bias2, low_s)
            xs = _attn_out(os_, zs, xs, wout, n_s)
            ks_l.append(kn.reshape(n_s, 1, N_HEADS, HEAD_DIM))
            vs_l.append(vn.reshape(n_s, 1, N_HEADS, HEAD_DIM))

    gf = final_norm_g.reshape(1, d)
    y_prompt = _final_norm(xp.reshape(bsz * lp, d), gf, ROW_TILE).reshape(bsz, lp, d)[:, N_META:seq_len]
    y_sample = _final_norm(xs, gf, n_s).reshape(n_s, 1, d)
    return (y_prompt, y_sample, jnp.stack(conv_p), jnp.stack(conv_s),
            jnp.stack(kp_l), jnp.stack(vp_l), jnp.stack(ks_l), jnp.stack(vs_l))
```

```python
import functools
import math

import jax
import jax.numpy as jnp
from jax import lax
from jax.experimental import pallas as pl
from jax.experimental.pallas import tpu as pltpu

D_MODEL = 1024
N_HEADS = 16
HEAD_DIM = 64
E_CONV = 2 * D_MODEL
CONV_W = 3
N_META = 16
EPS = 1e-6
LOG2E = 1.4426950408889634

SUBLANES = 8
LANES = 128
MXU_DIM = 256

ATT_TILE = MXU_DIM
ROW_TILE = 3 * ATT_TILE
IN_TILE = ATT_TILE
E_TILE = 512
QK_DIM = 2 * HEAD_DIM
TILES_PER_STEP = 4
VMEM_LIMIT = 56 * 1024 * 1024

_BF16 = jnp.bfloat16
_F32 = jnp.float32


def _params(n_axes):
    return pltpu.CompilerParams(
        dimension_semantics=("arbitrary",) * n_axes, vmem_limit_bytes=VMEM_LIMIT)


def _dot(a, b):
    return jnp.dot(a, b, preferred_element_type=_F32)


def _rms(x, g):
    ms = jnp.mean(x * x, axis=-1, keepdims=True)
    return x * lax.rsqrt(ms + EPS) * g


def _silu(z):
    return z * jax.nn.sigmoid(z)


def _softplus2(z):
    return jnp.maximum(z, 0.0) + jnp.log2(1.0 + jnp.exp2(-jnp.abs(z)))


def _conv_prompt_kernel(x_ref, g_ref, wb_ref, wc_ref, wx_ref, wz_ref, cw_ref, wout_ref, xo_ref, st_ref,
                        h_s, acc_s, carry_s, *, n_e, st_row):
    i = pl.program_id(1)
    e = pl.program_id(2)

    @pl.when(e == 0)
    def _():
        h_s[...] = _rms(x_ref[...], g_ref[...]).astype(_BF16)
        acc_s[...] = jnp.zeros_like(acc_s)

    @pl.when(i == 0)
    def _():
        carry_s[e] = jnp.zeros(carry_s.shape[1:], _F32)

    h = h_s[...]
    b = _dot(h, wb_ref[...])
    u = _dot(h, wc_ref[...]) * _dot(h, wx_ref[...])
    z = _dot(h, wz_ref[...])

    prev = carry_s[e]
    row = lax.broadcasted_iota(jnp.int32, u.shape, 0)
    u1 = jnp.where(row == 0, prev[7:8, :], pltpu.roll(u, 1, 0))
    u2 = jnp.where(row == 0, prev[6:7, :],
                   jnp.where(row == 1, prev[7:8, :], pltpu.roll(u, 2, 0)))
    tm = u.shape[0]
    carry_s[e] = u[tm - SUBLANES:, :]
    st_ref[...] = u[st_row:st_row + SUBLANES, :]

    conv = cw_ref[0:1, :] * u2 + cw_ref[1:2, :] * u1 + cw_ref[2:3, :] * u
    y = (b * conv * _silu(z)).astype(_BF16)
    acc_s[...] += _dot(y, wout_ref[...])

    @pl.when(e == n_e - 1)
    def _():
        xo_ref[...] = x_ref[...] + acc_s[...]


def _conv_prompt(x, g, win, cw, wout, seq_len):
    bsz, lp, d = x.shape
    e_dim = wout.shape[0]
    tm, te = ROW_TILE, E_TILE
    n_m, n_e = lp // tm, e_dim // te
    st_row = ((seq_len - 2) % tm) // SUBLANES * SUBLANES
    kern = functools.partial(_conv_prompt_kernel, n_e=n_e, st_row=st_row)
    x_new, st = pl.pallas_call(
        kern,
        grid=(bsz, n_m, n_e),
        in_specs=[
            pl.BlockSpec((None, tm, d), lambda b, i, e: (b, i, 0)),
            pl.BlockSpec((1, d), lambda b, i, e: (0, 0)),
            *[pl.BlockSpec((d, te), lambda b, i, e, part=part: (0, part * n_e + e)) for part in range(4)],
            pl.BlockSpec((CONV_W, te), lambda b, i, e: (0, e)),
            pl.BlockSpec((te, d), lambda b, i, e: (e, 0)),
        ],
        out_specs=[
            pl.BlockSpec((None, tm, d), lambda b, i, e: (b, i, 0)),
            pl.BlockSpec((None, None, SUBLANES, te), lambda b, i, e: (b, i, 0, e)),
        ],
        out_shape=[
            jax.ShapeDtypeStruct((bsz, lp, d), _F32),
            jax.ShapeDtypeStruct((bsz, n_m, SUBLANES, e_dim), _F32),
        ],
        scratch_shapes=[
            pltpu.VMEM((tm, d), _BF16),
            pltpu.VMEM((tm, d), _F32),
            pltpu.VMEM((n_e, SUBLANES, te), _F32),
        ],
        compiler_params=_params(3),
        name="conv_prompt",
    )(x, g, win, win, win, win, cw, wout)
    i_st = (seq_len - 2) // tm
    r = (seq_len - 2) % SUBLANES
    return x_new, st[:, i_st, r:r + 2, :]


def _conv_sample_kernel(x_ref, g_ref, wb_ref, wc_ref, wx_ref, wz_ref, cw_ref, wout_ref, p0_ref, p1_ref,
                        xo_ref, u_ref, h_s, acc_s, *, n_e):
    e = pl.program_id(0)

    @pl.when(e == 0)
    def _():
        h_s[...] = _rms(x_ref[...], g_ref[...]).astype(_BF16)
        acc_s[...] = jnp.zeros_like(acc_s)

    h = h_s[...]
    b = _dot(h, wb_ref[...])
    u = _dot(h, wc_ref[...]) * _dot(h, wx_ref[...])
    z = _dot(h, wz_ref[...])
    u_ref[...] = u
    conv = cw_ref[0:1, :] * p0_ref[...] + cw_ref[1:2, :] * p1_ref[...] + cw_ref[2:3, :] * u
    y = (b * conv * _silu(z)).astype(_BF16)
    acc_s[...] += _dot(y, wout_ref[...])

    @pl.when(e == n_e - 1)
    def _():
        xo_ref[...] = x_ref[...] + acc_s[...]


def _conv_sample(x, g, win, cw, wout, p0, p1):
    n, d = x.shape
    e_dim = wout.shape[0]
    te = E_TILE
    n_e = e_dim // te
    kern = functools.partial(_conv_sample_kernel, n_e=n_e)
    return pl.pallas_call(
        kern,
        grid=(n_e,),
        in_specs=[
            pl.BlockSpec((n, d), lambda e: (0, 0)),
            pl.BlockSpec((1, d), lambda e: (0, 0)),
            *[pl.BlockSpec((d, te), lambda e, part=part: (0, part * n_e + e)) for part in range(4)],
            pl.BlockSpec((CONV_W, te), lambda e: (0, e)),
            pl.BlockSpec((te, d), lambda e: (e, 0)),
            pl.BlockSpec((n, te), lambda e: (0, e)),
            pl.BlockSpec((n, te), lambda e: (0, e)),
        ],
        out_specs=[
            pl.BlockSpec((n, d), lambda e: (0, 0)),
            pl.BlockSpec((n, te), lambda e: (0, e)),
        ],
        out_shape=[
            jax.ShapeDtypeStruct((n, d), _F32),
            jax.ShapeDtypeStruct((n, e_dim), _F32),
        ],
        scratch_shapes=[pltpu.VMEM((n, d), _BF16), pltpu.VMEM((n, d), _F32)],
        compiler_params=_params(1),
        name="conv_sample",
    )(x, g, win, win, win, win, cw, wout, p0, p1)


def _attn_in_kernel(x_ref, g_ref, w_ref, btab_ref, *refs, n_prev):
    pk_ref, pv_ref = refs[:2] if n_prev else (None, None)
    kf_ref, vf_ref, z_ref, kp_ref, qt_ref, vt_ref = refs[-6:]
    _attn_in_body(x_ref, g_ref, w_ref, btab_ref, pk_ref, pv_ref, n_prev, kf_ref, vf_ref, z_ref, kp_ref, qt_ref, vt_ref)


def _attn_in_body(x_ref, g_ref, w_ref, btab_ref, pk_ref, pv_ref, n_prev, kf_ref, vf_ref, z_ref, kp_ref, qt_ref, vt_ref):
    d = x_ref.shape[-1]
    h = _rms(x_ref[...], g_ref[...]).astype(_BF16)
    q = _dot(h, w_ref[:, 0:d]) * (HEAD_DIM ** -0.5 * LOG2E)
    k = _dot(h, w_ref[:, d:2 * d])
    v = _dot(h, w_ref[:, 2 * d:3 * d])
    z_ref[...] = _dot(h, w_ref[:, 3 * d:4 * d])
    for a in range(n_prev):
        kf_ref[a] = pk_ref[a]
        vf_ref[a] = pv_ref[a]
    kf_ref[n_prev] = k
    vf_ref[n_prev] = v

    tm = k.shape[0]
    pair = 2 * HEAD_DIM
    lane = lax.broadcasted_iota(jnp.int32, (tm, pair), 1)
    slot = lax.broadcasted_iota(jnp.int32, (pair, tm), 0)
    ones_even = ((lane >= HEAD_DIM) & (lane < HEAD_DIM + 3)).astype(_F32)
    ones_odd = (lane < 3).astype(_F32)
    for hp in range(kp_ref.shape[0]):
        cols = slice(hp * pair, (hp + 1) * pair)
        kpair = k[:, cols]
        kp_ref[hp, 0] = jnp.where(lane < HEAD_DIM, kpair, ones_even).astype(_BF16)
        kp_ref[hp, 1] = jnp.where(lane >= HEAD_DIM, kpair, ones_odd).astype(_BF16)
        qpair_t = q[:, cols].T
        qt_ref[hp, 0] = jnp.where(slot < HEAD_DIM, qpair_t, btab_ref[2 * hp]).astype(_BF16)
        qt_ref[hp, 1] = jnp.where(slot >= HEAD_DIM, qpair_t, btab_ref[2 * hp + 1]).astype(_BF16)
        vt_ref[hp] = v[:, cols].T.astype(_BF16)


def _attn_in_prompt(x, g, w, btab, prev_kv, seq_len):
    bsz, lp, d = x.shape
    tm = IN_TILE
    n_t = lp // tm
    n_pairs = N_HEADS // 2
    pair = 2 * HEAD_DIM
    n_prev = 0 if prev_kv is None else prev_kv[0].shape[0]
    row = lambda b, i: (b, i, 0)
    stacked = lambda n: pl.BlockSpec((n, None, tm, d), lambda b, i: (0, b, i, 0))
    return pl.pallas_call(
        functools.partial(_attn_in_kernel, n_prev=n_prev),
        grid=(bsz, n_t),
        in_specs=[
            pl.BlockSpec((None, tm, d), row),
            pl.BlockSpec((1, d), lambda b, i: (0, 0)),
            pl.BlockSpec((d, 4 * d), lambda b, i: (0, 0)),
            pl.BlockSpec((N_HEADS, pair, tm), lambda b, i: (0, 0, 0)),
        ] + [stacked(n_prev)] * (2 if n_prev else 0),
        out_specs=[
            stacked(n_prev + 1),
            stacked(n_prev + 1),
            pl.BlockSpec((None, tm, d), row),
            pl.BlockSpec((None, n_pairs, None, 2, tm, pair), lambda b, i: (b, 0, i, 0, 0, 0)),
            pl.BlockSpec((None, n_pairs, None, 2, pair, tm), lambda b, i: (b, 0, i, 0, 0, 0)),
            pl.BlockSpec((None, n_pairs, None, pair, tm), lambda b, i: (b, 0, i, 0, 0)),
        ],
        out_shape=[
            jax.ShapeDtypeStruct((n_prev + 1, bsz, seq_len, d), _F32),
            jax.ShapeDtypeStruct((n_prev + 1, bsz, seq_len, d), _F32),
            jax.ShapeDtypeStruct((bsz, lp, d), _F32),
            jax.ShapeDtypeStruct((bsz, n_pairs, n_t, 2, tm, pair), _BF16),
            jax.ShapeDtypeStruct((bsz, n_pairs, n_t, 2, pair, tm), _BF16),
            jax.ShapeDtypeStruct((bsz, n_pairs, n_t, pair, tm), _BF16),
        ],
        compiler_params=_params(2),
        name="attn_in_prompt",
    )(x, g, w, btab, *(prev_kv or ()))


def _bias_table(bias, tq):
    parts = jnp.stack([p.astype(_F32) for p in _split3(bias.astype(_F32) * LOG2E)], axis=-1)
    first = jnp.where(jnp.arange(N_HEADS) % 2 == 0, HEAD_DIM, 0)
    slot = jnp.arange(2 * HEAD_DIM)[None, :] - first[:, None]
    tab = jnp.where((slot >= 0) & (slot < 3), jnp.take_along_axis(parts, jnp.clip(slot, 0, 2), axis=1), 0.0)
    return jnp.broadcast_to(tab[:, :, None], (N_HEADS, 2 * HEAD_DIM, tq))


def _attn_in_sample_kernel(x_ref, g_ref, w_ref, q_ref, k_ref, v_ref, z_ref):
    d = x_ref.shape[-1]
    h = _rms(x_ref[...], g_ref[...]).astype(_BF16)
    q_ref[...] = _dot(h, w_ref[:, 0:d]) * (HEAD_DIM ** -0.5 * LOG2E)
    k_ref[...] = _dot(h, w_ref[:, d:2 * d])
    v_ref[...] = _dot(h, w_ref[:, 2 * d:3 * d])
    z_ref[...] = _dot(h, w_ref[:, 3 * d:4 * d])


def _attn_in_sample(x, g, w):
    n, d = x.shape
    full = pl.BlockSpec((n, d), lambda i: (0, 0))
    return pl.pallas_call(
        _attn_in_sample_kernel,
        grid=(1,),
        in_specs=[full, pl.BlockSpec((1, d), lambda i: (0, 0)),
                  pl.BlockSpec((d, 4 * d), lambda i: (0, 0))],
        out_specs=[full] * 4,
        out_shape=[jax.ShapeDtypeStruct((n, d), _F32)] * 4,
        compiler_params=_params(1),
        name="attn_in_sample",
    )(x, g, w)


def _attn_out_kernel(o_ref, z_ref, x_ref, w_ref, xo_ref):
    y = (o_ref[...] * _silu(z_ref[...])).astype(_BF16)
    xo_ref[...] = x_ref[...] + _dot(y, w_ref[...])


def _attn_out(o, z, x, w, tm):
    rows, d = x.shape
    row = pl.BlockSpec((tm, d), lambda i: (i, 0))
    return pl.pallas_call(
        _attn_out_kernel,
        grid=(rows // tm,),
        in_specs=[row, row, row, pl.BlockSpec((d, d), lambda i: (0, 0))],
        out_specs=row,
        out_shape=jax.ShapeDtypeStruct((rows, d), _F32),
        compiler_params=_params(1),
        name="attn_out",
    )(o, z, x, w)


def _final_norm_kernel(x_ref, g_ref, y_ref):
    y_ref[...] = _rms(x_ref[...], g_ref[...])


def _final_norm(x, g, tm):
    rows, d = x.shape
    row = pl.BlockSpec((tm, d), lambda i: (i, 0))
    return pl.pallas_call(
        _final_norm_kernel,
        grid=(rows // tm,),
        in_specs=[row, pl.BlockSpec((1, d), lambda i: (0, 0))],
        out_specs=row,
        out_shape=jax.ShapeDtypeStruct((rows, d), _F32),
        compiler_params=_params(1),
        name="final_norm",
    )(x, g)


def _sb_prompt_kernel(kp_ref, qt_ref, vt_ref, u_ref, o_ref, za_ref, zb_ref, p_ref):
    qi = pl.program_id(2)
    n_heads = qt_ref.shape[0]
    tk = u_ref.shape[0]
    tq = qt_ref.shape[2]
    heads = range(n_heads)
    steps = range(TILES_PER_STEP)

    def fold(states, pending, slot):
        j0, n, cs = pending
        out = []
        pv = [[_dot(vt_ref[jnp.maximum(j0 - k, 0), pl.ds(hh * HEAD_DIM, HEAD_DIM), :],
                    p_ref[slot, hh * TILES_PER_STEP + k])
               for k in steps] for hh in heads]
        for hh in heads:
            acc, r = states[hh]
            for k in steps:
                live = (k < n).astype(_F32)
                acc = acc + (live * jnp.exp2(-r)) * pv[hh][k]
                r = r + live * cs[hh][k]
            out.append((acc, r))
        return tuple(out)

    def group(carry, z_ref, nxt_ref, slot, j0, n_tiles, first):
        states, pending = carry
        tiles = range(n_tiles)
        if first:
            for hh in heads:
                for k in tiles:
                    z_ref[hh * TILES_PER_STEP + k] = _dot(kp_ref[j0 - k, hh], qt_ref[hh])
        else:
            states = fold(states, pending, 1 - slot)
        z = [[z_ref[hh * TILES_PER_STEP + k] for k in tiles] for hh in heads]
        sp = [[_softplus2(z[hh][k]) for k in tiles] for hh in heads]
        if first:
            visible = (lax.broadcasted_iota(jnp.int32, (tk, tq), 0)
                       < lax.broadcasted_iota(jnp.int32, (tk, tq), 1))
            for hh in heads:
                sp[hh][0] = jnp.where(visible, sp[hh][0], 0.0)
        c = [[None for k in tiles] for hh in heads]
        for hh in heads:
            for k in steps:
                if k < n_tiles:
                    c[hh][k] = _dot(u_ref[...], sp[hh][k].astype(_BF16))
                nxt_ref[hh * TILES_PER_STEP + k] = _dot(
                    kp_ref[jnp.maximum(j0 - n_tiles - k, 0), hh], qt_ref[hh])
        sums = []
        for hh in heads:
            for k in steps:
                if k < n_tiles:
                    p = jnp.exp2(z_ref[hh * TILES_PER_STEP + k] - c[hh][k])
                    if first and k == 0:
                        p = jnp.where(visible, p, 0.0)
                else:
                    p = jnp.zeros((tk, tq), _F32)
                p_ref[slot, hh * TILES_PER_STEP + k] = p.astype(_BF16)
            sums.append(tuple(c[hh][k][0:1, :] if k < n_tiles else jnp.zeros((1, tq), _F32) for k in steps))
        return states, (j0, jnp.int32(n_tiles), tuple(sums))

    zero = tuple((jnp.zeros((HEAD_DIM, tq), _F32), jnp.zeros((1, tq), _F32)) for _ in heads)
    nothing = (qi, jnp.int32(0), tuple(tuple(jnp.zeros((1, tq), _F32) for _ in steps) for _ in heads))
    n_first = lax.rem(qi, TILES_PER_STEP) + 1
    n_more = qi // TILES_PER_STEP
    carry = lax.switch(
        n_first - 1,
        [functools.partial(group, z_ref=za_ref, nxt_ref=zb_ref, slot=0, j0=qi, n_tiles=n, first=True)
         for n in range(1, TILES_PER_STEP + 1)],
        (zero, nothing))

    def body(it, carry):
        j0 = qi - n_first - 2 * it * TILES_PER_STEP
        carry = group(carry, zb_ref, za_ref, 1, j0, TILES_PER_STEP, False)
        j1 = j0 - TILES_PER_STEP
        return lax.cond(j1 >= 0, lambda c: group(c, za_ref, zb_ref, 0, j1, TILES_PER_STEP, False),
                        lambda c: c, carry)

    states, pending = lax.fori_loop(0, (n_more + 1) // 2, body, carry)
    states = fold(states, pending, lax.rem(n_more, 2))
    o_ref[...] = jnp.concatenate([st[0] for st in states], axis=0).T


def _sb_prompt(kp, qt, vt, tri):
    bsz, n_pairs, n_k, _, tk, qk = kp.shape
    lp = n_k * tk
    tq = tk
    return pl.pallas_call(
        _sb_prompt_kernel,
        grid=(bsz, n_pairs, lp // tq),
        in_specs=[
            pl.BlockSpec((None, None, n_k, 2, tk, qk), lambda b, h, i: (b, h, 0, 0, 0, 0)),
            pl.BlockSpec((None, None, None, 2, qk, tq), lambda b, h, i: (b, h, i, 0, 0, 0)),
            pl.BlockSpec((None, None, n_k, 2 * HEAD_DIM, tk), lambda b, h, i: (b, h, 0, 0, 0)),
            pl.BlockSpec((tk, tk), lambda b, h, i: (0, 0)),
        ],
        out_specs=pl.BlockSpec((None, tq, 2 * HEAD_DIM), lambda b, h, i: (b, i, h)),
        out_shape=jax.ShapeDtypeStruct((bsz, lp, 2 * n_pairs * HEAD_DIM), _F32),
        scratch_shapes=[
            pltpu.VMEM((2 * TILES_PER_STEP, tk, tq), _F32),
            pltpu.VMEM((2 * TILES_PER_STEP, tk, tq), _F32),
            pltpu.VMEM((2, 2 * TILES_PER_STEP, tk, tq), _BF16),
        ],
        compiler_params=_params(3),
        name="sb_prompt",
    )(kp, qt, vt, tri)


def _split3(x):
    hi = x.astype(_BF16)
    mid = (x - hi.astype(_F32)).astype(_BF16)
    lo = (x - hi.astype(_F32) - mid.astype(_F32)).astype(_BF16)
    return hi, mid, lo


def _tri(n):
    r = lax.broadcasted_iota(jnp.int32, (n, n), 0)
    c = lax.broadcasted_iota(jnp.int32, (n, n), 1)
    return (c >= r).astype(_BF16)


def _sb_sample_kernel(pt_ref, *refs, n_pages):
    del pt_ref
    k_refs = refs[:n_pages]
    v_refs = refs[n_pages:2 * n_pages]
    q_ref, bias_ref, low_ref, o_ref, acc_s = refs[2 * n_pages:]
    hd, page = k_refs[0].shape
    qcol = jnp.transpose(jnp.broadcast_to(q_ref[...], (page, hd)))
    bias = bias_ref[...]
    low = low_ref[...]
    r = jnp.zeros((N_HEADS, page), _F32)
    acc_s[...] = jnp.zeros_like(acc_s)
    for p in reversed(range(n_pages)):
        z = (k_refs[p][...] * qcol).reshape(N_HEADS, HEAD_DIM, page).sum(axis=1) + bias
        sp = _softplus2(z)
        sp_hi = sp.astype(_BF16)
        sp_lo = (sp - sp_hi.astype(_F32)).astype(_BF16)
        c = _dot(sp_hi, low) + _dot(sp_lo, low)
        w = jnp.exp2(z - c - r)
        r = r + jnp.broadcast_to(c[:, 0:1], r.shape)
        wx = jnp.broadcast_to(w[:, None, :], (N_HEADS, HEAD_DIM, page)).reshape(hd, page)
        acc_s[...] += v_refs[p][...] * wx
    o_ref[...] = jnp.sum(jnp.transpose(acc_s[...]), axis=0, keepdims=True)


def _sb_sample(page_table, cache_kt, cache_vt, layer, q2, bias2, low):
    n, n_pages = page_table.shape
    _, _, hd, page = cache_kt.shape

    def page_spec(p):
        return pl.BlockSpec((None, None, hd, page), lambda b, pt, p=p: (layer, pt[b, p], 0, 0))

    in_specs = ([page_spec(p) for p in range(n_pages)] * 2 + [
        pl.BlockSpec((None, 1, hd), lambda b, pt: (b, 0, 0)),
        pl.BlockSpec((N_HEADS, page), lambda b, pt: (0, 0)),
        pl.BlockSpec((page, page), lambda b, pt: (0, 0)),
    ])
    out = pl.pallas_call(
        functools.partial(_sb_sample_kernel, n_pages=n_pages),
        grid_spec=pltpu.PrefetchScalarGridSpec(
            num_scalar_prefetch=1,
            grid=(n,),
            in_specs=in_specs,
            out_specs=pl.BlockSpec((None, 1, hd), lambda b, pt: (b, 0, 0)),
            scratch_shapes=[pltpu.VMEM((hd, page), _F32)],
        ),
        out_shape=jax.ShapeDtypeStruct((n, 1, hd), _F32),
        compiler_params=_params(1),
        name="sb_sample",
    )(page_table, *([cache_kt] * n_pages), *([cache_vt] * n_pages), q2, bias2, low)
    return out.reshape(n, hd)


def kernel(x_prompt, x_sample, state_conv, cache_k, cache_v, page_table, meta_tokens, norm_g, final_norm_g,
           w_in_conv, conv_w, w_out_conv, w_in_attn, w_out_attn, sb_bias):
    bsz, seq, d = x_prompt.shape
    n_s = x_sample.shape[0]
    seq_len = N_META + seq
    lp = -(-seq_len // ROW_TILE) * ROW_TILE
    assert lp - seq_len < IN_TILE, (lp, seq_len)
    depth = norm_g.shape[0]

    meta = jnp.broadcast_to(meta_tokens[None].astype(x_prompt.dtype), (bsz, N_META, d))
    xp = jnp.concatenate([meta, x_prompt, jnp.zeros((bsz, lp - seq_len, d), x_prompt.dtype)], axis=1)
    xs = x_sample.reshape(n_s, d)

    n_phys, page = cache_k.shape[1], cache_k.shape[2]
    ckt = cache_k.transpose(0, 1, 3, 4, 2).reshape(cache_k.shape[0], n_phys, N_HEADS * HEAD_DIM, page)
    cvt = cache_v.transpose(0, 1, 3, 4, 2).reshape(cache_v.shape[0], n_phys, N_HEADS * HEAD_DIM, page)
    tri_p = _tri(ATT_TILE)
    low_s = _tri(page).T

    conv_p, conv_s, ks_l, vs_l = [], [], [], []
    prompt_kv = None
    for i in range(depth):
        g = norm_g[i].reshape(1, d)
        if i % 2 == 0:
            c = i // 2
            win = w_in_conv[c].astype(_BF16)
            wout = w_out_conv[c].astype(_BF16)
            xp, stp = _conv_prompt(xp, g, win, conv_w[c], wout, seq_len)
            xs, u_s = _conv_sample(xs, g, win, conv_w[c], wout, state_conv[c, :, 0, :], state_conv[c, :, 1, :])
            conv_p.append(stp)
            conv_s.append(jnp.stack([state_conv[c, :, 1, :], u_s], axis=1))
        else:
            a = i // 2
            w4 = w_in_attn[a].astype(_BF16)
            wout = w_out_attn[a].astype(_BF16)
            kf, vf, zp, kp, qt, vt = _attn_in_prompt(
                xp, g, w4, _bias_table(sb_bias[a], IN_TILE), prompt_kv, seq_len)
            prompt_kv = (kf, vf)
            op = _sb_prompt(kp, qt, vt, tri_p)
            xp = _attn_out(op.reshape(bsz * lp, d), zp.reshape(bsz * lp, d), xp.reshape(bsz * lp, d),
                           wout, ROW_TILE).reshape(bsz, lp, d)
            q2, kn, vn, zs = _attn_in_sample(xs, g, w4)
            bias2 = jnp.broadcast_to((sb_bias[a].astype(_F32) * LOG2E)[:, None], (N_HEADS, page))
            os_ = _sb_sample(page_table, ckt, cvt, a, q2.reshape(n_s, 1, d), bias2, low_s)
            xs = _attn_out(os_, zs, xs, wout, n_s)
            ks_l.append(kn.reshape(n_s, 1, N_HEADS, HEAD_DIM))
            vs_l.append(vn.reshape(n_s, 1, N_HEADS, HEAD_DIM))

    gf = final_norm_g.reshape(1, d)
    y_prompt = _final_norm(xp.reshape(bsz * lp, d), gf, ROW_TILE).reshape(bsz, lp, d)[:, N_META:seq_len]
    y_sample = _final_norm(xs, gf, n_s).reshape(n_s, 1, d)
    new_k, new_v = (t.reshape(t.shape[0], bsz, seq_len, N_HEADS, HEAD_DIM) for t in prompt_kv)
    return (y_prompt, y_sample, jnp.stack(conv_p), jnp.stack(conv_s),
            new_k, new_v, jnp.stack(ks_l), jnp.stack(vs_l))
```

```python
import functools

import jax
import jax.numpy as jnp
from jax import lax
from jax.experimental import pallas as pl
from jax.experimental.pallas import tpu as pltpu

D_MODEL = 1024
N_HEADS = 16
HEAD_DIM = 64
E_CONV = 2 * D_MODEL
CONV_W = 3
N_META = 16
EPS = 1e-6
LOG2E = 1.4426950408889634

SUBLANES = 8
MXU_DIM = 256

ATT_TILE = MXU_DIM
ROW_TILE = 3 * ATT_TILE
IN_TILE = ATT_TILE
E_TILE = 1024
TILES_PER_STEP = 4
VMEM_LIMIT = 56 * 1024 * 1024

_BF16 = jnp.bfloat16
_F32 = jnp.float32


def _params(n_axes):
    return pltpu.CompilerParams(
        dimension_semantics=("arbitrary",) * n_axes, vmem_limit_bytes=VMEM_LIMIT)


def _dot(a, b):
    return jnp.dot(a, b, preferred_element_type=_F32)


def _rms(x, g):
    ms = jnp.mean(x * x, axis=-1, keepdims=True)
    return x * lax.rsqrt(ms + EPS) * g


def _silu(z):
    return z * jax.nn.sigmoid(z)


def _softplus2(z):
    return jnp.maximum(z, 0.0) + jnp.log2(1.0 + jnp.exp2(-jnp.abs(z)))


def _conv_prompt_kernel(x_ref, g_ref, wb_ref, wc_ref, wx_ref, wz_ref, cw_ref, wout_ref, xo_ref, st_ref,
                        h_s, acc_s, carry_s, *, n_e, st_row):
    i = pl.program_id(1)
    e = pl.program_id(2)

    @pl.when(e == 0)
    def _():
        h_s[...] = _rms(x_ref[...], g_ref[...]).astype(_BF16)
        acc_s[...] = jnp.zeros_like(acc_s)

    @pl.when(i == 0)
    def _():
        carry_s[e] = jnp.zeros(carry_s.shape[1:], _F32)

    h = h_s[...]
    b = _dot(h, wb_ref[...])
    u = _dot(h, wc_ref[...]) * _dot(h, wx_ref[...])
    z = _dot(h, wz_ref[...])

    prev = carry_s[e]
    row = lax.broadcasted_iota(jnp.int32, u.shape, 0)
    u1 = jnp.where(row == 0, prev[7:8, :], pltpu.roll(u, 1, 0))
    u2 = jnp.where(row == 0, prev[6:7, :],
                   jnp.where(row == 1, prev[7:8, :], pltpu.roll(u, 2, 0)))
    tm = u.shape[0]
    carry_s[e] = u[tm - SUBLANES:, :]
    st_ref[...] = u[st_row:st_row + SUBLANES, :]

    conv = cw_ref[0:1, :] * u2 + cw_ref[1:2, :] * u1 + cw_ref[2:3, :] * u
    y = (b * conv * _silu(z)).astype(_BF16)
    acc_s[...] += _dot(y, wout_ref[...])

    @pl.when(e == n_e - 1)
    def _():
        xo_ref[...] = x_ref[...] + acc_s[...]


def _conv_prompt(x, g, win, cw, wout, seq_len):
    bsz, lp, d = x.shape
    e_dim = wout.shape[0]
    tm, te = ROW_TILE, E_TILE
    n_m, n_e = lp // tm, e_dim // te
    st_row = ((seq_len - 2) % tm) // SUBLANES * SUBLANES
    kern = functools.partial(_conv_prompt_kernel, n_e=n_e, st_row=st_row)
    x_new, st = pl.pallas_call(
        kern,
        grid=(bsz, n_m, n_e),
        in_specs=[
            pl.BlockSpec((None, tm, d), lambda b, i, e: (b, i, 0)),
            pl.BlockSpec((1, d), lambda b, i, e: (0, 0)),
            *[pl.BlockSpec((d, te), lambda b, i, e, part=part: (0, part * n_e + e)) for part in range(4)],
            pl.BlockSpec((CONV_W, te), lambda b, i, e: (0, e)),
            pl.BlockSpec((te, d), lambda b, i, e: (e, 0)),
        ],
        out_specs=[
            pl.BlockSpec((None, tm, d), lambda b, i, e: (b, i, 0)),
            pl.BlockSpec((None, None, SUBLANES, te), lambda b, i, e: (b, i, 0, e)),
        ],
        out_shape=[
            jax.ShapeDtypeStruct((bsz, lp, d), _F32),
            jax.ShapeDtypeStruct((bsz, n_m, SUBLANES, e_dim), _F32),
        ],
        scratch_shapes=[
            pltpu.VMEM((tm, d), _BF16),
            pltpu.VMEM((tm, d), _F32),
            pltpu.VMEM((n_e, SUBLANES, te), _F32),
        ],
        compiler_params=_params(3),
        name="conv_prompt",
    )(x, g, win, win, win, win, cw, wout)
    i_st = (seq_len - 2) // tm
    r = (seq_len - 2) % SUBLANES
    return x_new, st[:, i_st, r:r + 2, :]


def _conv_sample_kernel(x_ref, g_ref, wb_ref, wc_ref, wx_ref, wz_ref, cw_ref, wout_ref, p0_ref, p1_ref,
                        xo_ref, u_ref, h_s, acc_s, *, n_e):
    e = pl.program_id(0)

    @pl.when(e == 0)
    def _():
        h_s[...] = _rms(x_ref[...], g_ref[...]).astype(_BF16)
        acc_s[...] = jnp.zeros_like(acc_s)

    h = h_s[...]
    b = _dot(h, wb_ref[...])
    u = _dot(h, wc_ref[...]) * _dot(h, wx_ref[...])
    z = _dot(h, wz_ref[...])
    u_ref[...] = u
    conv = cw_ref[0:1, :] * p0_ref[...] + cw_ref[1:2, :] * p1_ref[...] + cw_ref[2:3, :] * u
    y = (b * conv * _silu(z)).astype(_BF16)
    acc_s[...] += _dot(y, wout_ref[...])

    @pl.when(e == n_e - 1)
    def _():
        xo_ref[...] = x_ref[...] + acc_s[...]


def _conv_sample(x, g, win, cw, wout, p0, p1):
    n, d = x.shape
    e_dim = wout.shape[0]
    te = E_TILE
    n_e = e_dim // te
    kern = functools.partial(_conv_sample_kernel, n_e=n_e)
    return pl.pallas_call(
        kern,
        grid=(n_e,),
        in_specs=[
            pl.BlockSpec((n, d), lambda e: (0, 0)),
            pl.BlockSpec((1, d), lambda e: (0, 0)),
            *[pl.BlockSpec((d, te), lambda e, part=part: (0, part * n_e + e)) for part in range(4)],
            pl.BlockSpec((CONV_W, te), lambda e: (0, e)),
            pl.BlockSpec((te, d), lambda e: (e, 0)),
            pl.BlockSpec((n, te), lambda e: (0, e)),
            pl.BlockSpec((n, te), lambda e: (0, e)),
        ],
        out_specs=[
            pl.BlockSpec((n, d), lambda e: (0, 0)),
            pl.BlockSpec((n, te), lambda e: (0, e)),
        ],
        out_shape=[
            jax.ShapeDtypeStruct((n, d), _F32),
            jax.ShapeDtypeStruct((n, e_dim), _F32),
        ],
        scratch_shapes=[pltpu.VMEM((n, d), _BF16), pltpu.VMEM((n, d), _F32)],
        compiler_params=_params(1),
        name="conv_sample",
    )(x, g, win, win, win, win, cw, wout, p0, p1)


def _attn_in_kernel(x_ref, g_ref, w_ref, btab_ref, *refs, n_prev):
    pk_ref, pv_ref = refs[:2] if n_prev else (None, None)
    kf_ref, vf_ref, z_ref, kp_ref, qt_ref, vt_ref = refs[-6:]
    _attn_in_body(x_ref, g_ref, w_ref, btab_ref, pk_ref, pv_ref, n_prev, kf_ref, vf_ref, z_ref, kp_ref, qt_ref, vt_ref)


def _attn_in_body(x_ref, g_ref, w_ref, btab_ref, pk_ref, pv_ref, n_prev, kf_ref, vf_ref, z_ref, kp_ref, qt_ref, vt_ref):
    d = x_ref.shape[-1]
    h = _rms(x_ref[...], g_ref[...]).astype(_BF16)
    q = _dot(h, w_ref[:, 0:d]) * (HEAD_DIM ** -0.5 * LOG2E)
    k = _dot(h, w_ref[:, d:2 * d])
    v = _dot(h, w_ref[:, 2 * d:3 * d])
    z_ref[...] = _dot(h, w_ref[:, 3 * d:4 * d])
    for a in range(n_prev):
        kf_ref[a] = pk_ref[a]
        vf_ref[a] = pv_ref[a]
    kf_ref[n_prev] = k
    vf_ref[n_prev] = v

    tm = k.shape[0]
    pair = 2 * HEAD_DIM
    lane = lax.broadcasted_iota(jnp.int32, (tm, pair), 1)
    slot = lax.broadcasted_iota(jnp.int32, (pair, tm), 0)
    ones_even = ((lane >= HEAD_DIM) & (lane < HEAD_DIM + 3)).astype(_F32)
    ones_odd = (lane < 3).astype(_F32)
    for hp in range(kp_ref.shape[0]):
        cols = slice(hp * pair, (hp + 1) * pair)
        kpair = k[:, cols]
        kp_ref[hp, 0] = jnp.where(lane < HEAD_DIM, kpair, ones_even).astype(_BF16)
        kp_ref[hp, 1] = jnp.where(lane >= HEAD_DIM, kpair, ones_odd).astype(_BF16)
        qpair_t = q[:, cols].T
        qt_ref[hp, 0] = jnp.where(slot < HEAD_DIM, qpair_t, btab_ref[2 * hp]).astype(_BF16)
        qt_ref[hp, 1] = jnp.where(slot >= HEAD_DIM, qpair_t, btab_ref[2 * hp + 1]).astype(_BF16)
        vt_ref[hp] = v[:, cols].T.astype(_BF16)


def _attn_in_prompt(x, g, w, btab, prev_kv, seq_len):
    bsz, lp, d = x.shape
    tm = IN_TILE
    n_t = lp // tm
    n_pairs = N_HEADS // 2
    pair = 2 * HEAD_DIM
    n_prev = 0 if prev_kv is None else prev_kv[0].shape[0]
    row = lambda b, i: (b, i, 0)
    stacked = lambda n: pl.BlockSpec((n, None, tm, d), lambda b, i: (0, b, i, 0))
    return pl.pallas_call(
        functools.partial(_attn_in_kernel, n_prev=n_prev),
        grid=(bsz, n_t),
        in_specs=[
            pl.BlockSpec((None, tm, d), row),
            pl.BlockSpec((1, d), lambda b, i: (0, 0)),
            pl.BlockSpec((d, 4 * d), lambda b, i: (0, 0)),
            pl.BlockSpec((N_HEADS, pair, tm), lambda b, i: (0, 0, 0)),
        ] + [stacked(n_prev)] * (2 if n_prev else 0),
        out_specs=[
            stacked(n_prev + 1),
            stacked(n_prev + 1),
            pl.BlockSpec((None, tm, d), row),
            pl.BlockSpec((None, n_pairs, None, 2, tm, pair), lambda b, i: (b, 0, i, 0, 0, 0)),
            pl.BlockSpec((None, n_pairs, None, 2, pair, tm), lambda b, i: (b, 0, i, 0, 0, 0)),
            pl.BlockSpec((None, n_pairs, None, pair, tm), lambda b, i: (b, 0, i, 0, 0)),
        ],
        out_shape=[
            jax.ShapeDtypeStruct((n_prev + 1, bsz, seq_len, d), _F32),
            jax.ShapeDtypeStruct((n_prev + 1, bsz, seq_len, d), _F32),
            jax.ShapeDtypeStruct((bsz, lp, d), _F32),
            jax.ShapeDtypeStruct((bsz, n_pairs, n_t, 2, tm, pair), _BF16),
            jax.ShapeDtypeStruct((bsz, n_pairs, n_t, 2, pair, tm), _BF16),
            jax.ShapeDtypeStruct((bsz, n_pairs, n_t, pair, tm), _BF16),
        ],
        compiler_params=_params(2),
        name="attn_in_prompt",
    )(x, g, w, btab, *(prev_kv or ()))


def _bias_table(bias, tq):
    parts = jnp.stack([p.astype(_F32) for p in _split3(bias.astype(_F32) * LOG2E)], axis=-1)
    first = jnp.where(jnp.arange(N_HEADS) % 2 == 0, HEAD_DIM, 0)
    slot = jnp.arange(2 * HEAD_DIM)[None, :] - first[:, None]
    tab = jnp.where((slot >= 0) & (slot < 3), jnp.take_along_axis(parts, jnp.clip(slot, 0, 2), axis=1), 0.0)
    return jnp.broadcast_to(tab[:, :, None], (N_HEADS, 2 * HEAD_DIM, tq))


def _attn_in_sample_kernel(x_ref, g_ref, w_ref, q_ref, k_ref, v_ref, z_ref):
    d = x_ref.shape[-1]
    h = _rms(x_ref[...], g_ref[...]).astype(_BF16)
    q_ref[...] = _dot(h, w_ref[:, 0:d]) * (HEAD_DIM ** -0.5 * LOG2E)
    k_ref[...] = _dot(h, w_ref[:, d:2 * d])
    v_ref[...] = _dot(h, w_ref[:, 2 * d:3 * d])
    z_ref[...] = _dot(h, w_ref[:, 3 * d:4 * d])


def _attn_in_sample(x, g, w):
    n, d = x.shape
    full = pl.BlockSpec((n, d), lambda i: (0, 0))
    return pl.pallas_call(
        _attn_in_sample_kernel,
        grid=(1,),
        in_specs=[full, pl.BlockSpec((1, d), lambda i: (0, 0)),
                  pl.BlockSpec((d, 4 * d), lambda i: (0, 0))],
        out_specs=[full] * 4,
        out_shape=[jax.ShapeDtypeStruct((n, d), _F32)] * 4,
        compiler_params=_params(1),
        name="attn_in_sample",
    )(x, g, w)


def _attn_out_kernel(o_ref, z_ref, x_ref, w_ref, xo_ref):
    y = (o_ref[...] * _silu(z_ref[...])).astype(_BF16)
    xo_ref[...] = x_ref[...] + _dot(y, w_ref[...])


def _attn_out(o, z, x, w, tm):
    rows, d = x.shape
    row = pl.BlockSpec((tm, d), lambda i: (i, 0))
    return pl.pallas_call(
        _attn_out_kernel,
        grid=(rows // tm,),
        in_specs=[row, row, row, pl.BlockSpec((d, d), lambda i: (0, 0))],
        out_specs=row,
        out_shape=jax.ShapeDtypeStruct((rows, d), _F32),
        compiler_params=_params(1),
        name="attn_out",
    )(o, z, x, w)


def _final_norm_kernel(x_ref, g_ref, y_ref):
    y_ref[...] = _rms(x_ref[...], g_ref[...])


def _final_norm(x, g, tm):
    rows, d = x.shape
    row = pl.BlockSpec((tm, d), lambda i: (i, 0))
    return pl.pallas_call(
        _final_norm_kernel,
        grid=(rows // tm,),
        in_specs=[row, pl.BlockSpec((1, d), lambda i: (0, 0))],
        out_specs=row,
        out_shape=jax.ShapeDtypeStruct((rows, d), _F32),
        compiler_params=_params(1),
        name="final_norm",
    )(x, g)


def _sb_prompt_kernel(kp_ref, qt_ref, vt_ref, u_ref, o_ref, za_ref, zb_ref, p_ref):
    qi = pl.program_id(2)
    n_heads = qt_ref.shape[0]
    tk = u_ref.shape[0]
    tq = qt_ref.shape[2]
    heads = range(n_heads)
    steps = range(TILES_PER_STEP)

    def fold(states, pending, slot):
        j0, n, cs = pending
        out = []
        pv = [[_dot(vt_ref[jnp.maximum(j0 - k, 0), pl.ds(hh * HEAD_DIM, HEAD_DIM), :],
                    p_ref[slot, hh * TILES_PER_STEP + k])
               for k in steps] for hh in heads]
        for hh in heads:
            acc, r = states[hh]
            for k in steps:
                live = (k < n).astype(_F32)
                acc = acc + (live * jnp.exp2(-r)) * pv[hh][k]
                r = r + live * cs[hh][k]
            out.append((acc, r))
        return tuple(out)

    def group(carry, z_ref, nxt_ref, slot, j0, n_tiles, first):
        states, pending = carry
        tiles = range(n_tiles)
        if first:
            for hh in heads:
                for k in tiles:
                    z_ref[hh * TILES_PER_STEP + k] = _dot(kp_ref[j0 - k, hh], qt_ref[hh])
        else:
            states = fold(states, pending, 1 - slot)
        z = [[z_ref[hh * TILES_PER_STEP + k] for k in tiles] for hh in heads]
        sp = [[_softplus2(z[hh][k]) for k in tiles] for hh in heads]
        if first:
            visible = (lax.broadcasted_iota(jnp.int32, (tk, tq), 0)
                       < lax.broadcasted_iota(jnp.int32, (tk, tq), 1))
            for hh in heads:
                sp[hh][0] = jnp.where(visible, sp[hh][0], 0.0)
        c = [[None for k in tiles] for hh in heads]
        for hh in heads:
            for k in steps:
                if k < n_tiles:
                    c[hh][k] = _dot(u_ref[...], sp[hh][k].astype(_BF16))
                nxt_ref[hh * TILES_PER_STEP + k] = _dot(
                    kp_ref[jnp.maximum(j0 - n_tiles - k, 0), hh], qt_ref[hh])
        sums = []
        for hh in heads:
            for k in steps:
                if k < n_tiles:
                    p = jnp.exp2(z_ref[hh * TILES_PER_STEP + k] - c[hh][k])
                    if first and k == 0:
                        p = jnp.where(visible, p, 0.0)
                else:
                    p = jnp.zeros((tk, tq), _F32)
                p_ref[slot, hh * TILES_PER_STEP + k] = p.astype(_BF16)
            sums.append(tuple(c[hh][k][0:1, :] if k < n_tiles else jnp.zeros((1, tq), _F32) for k in steps))
        return states, (j0, jnp.int32(n_tiles), tuple(sums))

    zero = tuple((jnp.zeros((HEAD_DIM, tq), _F32), jnp.zeros((1, tq), _F32)) for _ in heads)
    nothing = (qi, jnp.int32(0), tuple(tuple(jnp.zeros((1, tq), _F32) for _ in steps) for _ in heads))
    n_first = lax.rem(qi, TILES_PER_STEP) + 1
    n_more = qi // TILES_PER_STEP
    carry = lax.switch(
        n_first - 1,
        [functools.partial(group, z_ref=za_ref, nxt_ref=zb_ref, slot=0, j0=qi, n_tiles=n, first=True)
         for n in range(1, TILES_PER_STEP + 1)],
        (zero, nothing))

    def body(it, carry):
        j0 = qi - n_first - 2 * it * TILES_PER_STEP
        carry = group(carry, zb_ref, za_ref, 1, j0, TILES_PER_STEP, False)
        j1 = j0 - TILES_PER_STEP
        return lax.cond(j1 >= 0, lambda c: group(c, za_ref, zb_ref, 0, j1, TILES_PER_STEP, False),
                        lambda c: c, carry)

    states, pending = lax.fori_loop(0, (n_more + 1) // 2, body, carry)
    states = fold(states, pending, lax.rem(n_more, 2))
    o_ref[...] = jnp.concatenate([st[0] for st in states], axis=0).T


def _sb_prompt(kp, qt, vt, tri):
    bsz, n_pairs, n_k, _, tk, qk = kp.shape
    lp = n_k * tk
    tq = tk
    return pl.pallas_call(
        _sb_prompt_kernel,
        grid=(bsz, n_pairs, lp // tq),
        in_specs=[
            pl.BlockSpec((None, None, n_k, 2, tk, qk), lambda b, h, i: (b, h, 0, 0, 0, 0)),
            pl.BlockSpec((None, None, None, 2, qk, tq), lambda b, h, i: (b, h, i, 0, 0, 0)),
            pl.BlockSpec((None, None, n_k, 2 * HEAD_DIM, tk), lambda b, h, i: (b, h, 0, 0, 0)),
            pl.BlockSpec((tk, tk), lambda b, h, i: (0, 0)),
        ],
        out_specs=pl.BlockSpec((None, tq, 2 * HEAD_DIM), lambda b, h, i: (b, i, h)),
        out_shape=jax.ShapeDtypeStruct((bsz, lp, 2 * n_pairs * HEAD_DIM), _F32),
        scratch_shapes=[
            pltpu.VMEM((2 * TILES_PER_STEP, tk, tq), _F32),
            pltpu.VMEM((2 * TILES_PER_STEP, tk, tq), _F32),
            pltpu.VMEM((2, 2 * TILES_PER_STEP, tk, tq), _BF16),
        ],
        compiler_params=_params(3),
        name="sb_prompt",
    )(kp, qt, vt, tri)


def _split3(x):
    hi = x.astype(_BF16)
    mid = (x - hi.astype(_F32)).astype(_BF16)
    lo = (x - hi.astype(_F32) - mid.astype(_F32)).astype(_BF16)
    return hi, mid, lo


def _tri(n):
    r = lax.broadcasted_iota(jnp.int32, (n, n), 0)
    c = lax.broadcasted_iota(jnp.int32, (n, n), 1)
    return (c >= r).astype(_BF16)


def _sb_sample_kernel(pt_ref, *refs, n_pages):
    del pt_ref
    k_refs = refs[:n_pages]
    v_refs = refs[n_pages:2 * n_pages]
    q_ref, bias_ref, low_ref, o_ref, acc_s = refs[2 * n_pages:]
    hd, page = k_refs[0].shape
    qcol = jnp.transpose(jnp.broadcast_to(q_ref[...], (page, hd)))
    bias = bias_ref[...]
    low = low_ref[...]
    r = jnp.zeros((N_HEADS, page), _F32)
    acc_s[...] = jnp.zeros_like(acc_s)
    for p in reversed(range(n_pages)):
        z = (k_refs[p][...] * qcol).reshape(N_HEADS, HEAD_DIM, page).sum(axis=1) + bias
        sp = _softplus2(z)
        sp_hi = sp.astype(_BF16)
        sp_lo = (sp - sp_hi.astype(_F32)).astype(_BF16)
        c = _dot(sp_hi, low) + _dot(sp_lo, low)
        w = jnp.exp2(z - c - r)
        r = r + jnp.broadcast_to(c[:, 0:1], r.shape)
        wx = jnp.broadcast_to(w[:, None, :], (N_HEADS, HEAD_DIM, page)).reshape(hd, page)
        acc_s[...] += v_refs[p][...] * wx
    o_ref[...] = jnp.sum(jnp.transpose(acc_s[...]), axis=0, keepdims=True)


def _sb_sample(page_table, cache_kt, cache_vt, layer, q2, bias2, low):
    n, n_pages = page_table.shape
    _, _, hd, page = cache_kt.shape

    def page_spec(p):
        return pl.BlockSpec((None, None, hd, page), lambda b, pt, p=p: (layer, pt[b, p], 0, 0))

    in_specs = ([page_spec(p) for p in range(n_pages)] * 2 + [
        pl.BlockSpec((None, 1, hd), lambda b, pt: (b, 0, 0)),
        pl.BlockSpec((N_HEADS, page), lambda b, pt: (0, 0)),
        pl.BlockSpec((page, page), lambda b, pt: (0, 0)),
    ])
    out = pl.pallas_call(
        functools.partial(_sb_sample_kernel, n_pages=n_pages),
        grid_spec=pltpu.PrefetchScalarGridSpec(
            num_scalar_prefetch=1,
            grid=(n,),
            in_specs=in_specs,
            out_specs=pl.BlockSpec((None, 1, hd), lambda b, pt: (b, 0, 0)),
            scratch_shapes=[pltpu.VMEM((hd, page), _F32)],
        ),
        out_shape=jax.ShapeDtypeStruct((n, 1, hd), _F32),
        compiler_params=_params(1),
        name="sb_sample",
    )(page_table, *([cache_kt] * n_pages), *([cache_vt] * n_pages), q2, bias2, low)
    return out.reshape(n, hd)


def kernel(x_prompt, x_sample, state_conv, cache_k, cache_v, page_table, meta_tokens, norm_g, final_norm_g,
           w_in_conv, conv_w, w_out_conv, w_in_attn, w_out_attn, sb_bias):
    bsz, seq, d = x_prompt.shape
    n_s = x_sample.shape[0]
    seq_len = N_META + seq
    lp = -(-seq_len // ROW_TILE) * ROW_TILE
    assert lp - seq_len < IN_TILE, (lp, seq_len)
    depth = norm_g.shape[0]

    meta = jnp.broadcast_to(meta_tokens[None].astype(x_prompt.dtype), (bsz, N_META, d))
    xp = jnp.concatenate([meta, x_prompt, jnp.zeros((bsz, lp - seq_len, d), x_prompt.dtype)], axis=1)
    xs = x_sample.reshape(n_s, d)

    n_phys, page = cache_k.shape[1], cache_k.shape[2]
    ckt = cache_k.transpose(0, 1, 3, 4, 2).reshape(cache_k.shape[0], n_phys, N_HEADS * HEAD_DIM, page)
    cvt = cache_v.transpose(0, 1, 3, 4, 2).reshape(cache_v.shape[0], n_phys, N_HEADS * HEAD_DIM, page)
    tri_p = _tri(ATT_TILE)
    low_s = _tri(page).T

    conv_p, conv_s, ks_l, vs_l = [], [], [], []
    prompt_kv = None
    for i in range(depth):
        g = norm_g[i].reshape(1, d)
        if i % 2 == 0:
            c = i // 2
            win = w_in_conv[c].astype(_BF16)
            wout = w_out_conv[c].astype(_BF16)
            xp, stp = _conv_prompt(xp, g, win, conv_w[c], wout, seq_len)
            xs, u_s = _conv_sample(xs, g, win, conv_w[c], wout, state_conv[c, :, 0, :], state_conv[c, :, 1, :])
            conv_p.append(stp)
            conv_s.append(jnp.stack([state_conv[c, :, 1, :], u_s], axis=1))
        else:
            a = i // 2
            w4 = w_in_attn[a].astype(_BF16)
            wout = w_out_attn[a].astype(_BF16)
            kf, vf, zp, kp, qt, vt = _attn_in_prompt(
                xp, g, w4, _bias_table(sb_bias[a], IN_TILE), prompt_kv, seq_len)
            prompt_kv = (kf, vf)
            op = _sb_prompt(kp, qt, vt, tri_p)
            xp = _attn_out(op.reshape(bsz * lp, d), zp.reshape(bsz * lp, d), xp.reshape(bsz * lp, d),
                           wout, ROW_TILE).reshape(bsz, lp, d)
            q2, kn, vn, zs = _attn_in_sample(xs, g, w4)
            bias2 = jnp.broadcast_to((sb_bias[a].astype(_F32) * LOG2E)[:, None], (N_HEADS, page))
            os_ = _sb_sample(page_table, ckt, cvt, a, q2.reshape(n_s, 1, d), bias2, low_s)
            xs = _attn_out(os_, zs, xs, wout, n_s)
            ks_l.append(kn.reshape(n_s, 1, N_HEADS, HEAD_DIM))
            vs_l.append(vn.reshape(n_s, 1, N_HEADS, HEAD_DIM))

    gf = final_norm_g.reshape(1, d)
    y_prompt = _final_norm(xp.reshape(bsz * lp, d), gf, ROW_TILE).reshape(bsz, lp, d)[:, N_META:seq_len]
    y_sample = _final_norm(xs, gf, n_s).reshape(n_s, 1, d)
    new_k, new_v = (t.reshape(t.shape[0], bsz, seq_len, N_HEADS, HEAD_DIM) for t in prompt_kv)
    return (y_prompt, y_sample, jnp.stack(conv_p), jnp.stack(conv_s),
            new_k, new_v, jnp.stack(ks_l), jnp.stack(vs_l))
```
